```python
import math
import jax, jax.numpy as jnp
from jax import lax
import numpy as np

D_MODEL = 2048
BATCH = 1
SEQ = 16384
DEPTH = 1
DEC_BATCH = 32
DEC_SEQ = 8
PAST_LEN = 16384
PAGE_SIZE = 128

SB_HEADS = 8
SB_HEAD_DIM = 128
SB_WIDTH = SB_HEADS * SB_HEAD_DIM
SB_BIAS_MAX = -3.0
SB_BIAS_MIN = -11.0
GDN_HEADS = 8
GDN_KEY_DIM = 128
GDN_VAL_DIM = 128
GDN_QK_WIDTH = GDN_HEADS * GDN_KEY_DIM
GDN_V_WIDTH = GDN_HEADS * GDN_VAL_DIM
CONV_WIDTH = 4
CONV_CH = 2 * GDN_QK_WIDTH + GDN_V_WIDTH
GDN_CHUNK = 64
Q_BLOCK = 128
D_FF = -(-(8 * D_MODEL) // (3 * 256)) * 256
IN_SPLITS = (SB_WIDTH, SB_WIDTH, SB_WIDTH, CONV_CH, GDN_V_WIDTH, GDN_HEADS, GDN_HEADS, D_MODEL, D_MODEL)
N_IN = sum(IN_SPLITS)
DEEPNORM_ALPHA = (2.0 * DEPTH) ** 0.25
DEEPNORM_BETA = (8.0 * DEPTH) ** -0.25
LN_EPS = 1e-5
RMS_EPS = 1e-6
L2_EPS = 1e-6

kernel_name = "stickbreak_gdn_hybrid_step"


def _split_columns(p):
    outs, start = [], 0
    for w in IN_SPLITS:
        outs.append(p[..., start:start + w])
        start += w
    return outs


def _layernorm(x, g, b):
    xf = x.astype(jnp.float32)
    mu = xf.mean(-1, keepdims=True)
    var = jnp.square(xf - mu).mean(-1, keepdims=True)
    return ((xf - mu) * lax.rsqrt(var + LN_EPS) * g + b).astype(x.dtype)


def _rmsnorm(x):
    xf = x.astype(jnp.float32)
    return xf * lax.rsqrt(jnp.mean(xf * xf, -1, keepdims=True) + RMS_EPS)


def _l2norm(x):
    xf = x.astype(jnp.float32)
    return xf * lax.rsqrt(jnp.sum(xf * xf, -1, keepdims=True) + L2_EPS)


def _causal_conv(x_ext, w):
    t = x_ext.shape[1] - (CONV_WIDTH - 1)
    out = x_ext[:, 0:t] * w[0]
    for i in range(1, CONV_WIDTH):
        out = out + x_ext[:, i:i + t] * w[i]
    return out


def _sb_weights(z, mask, acc):
    L = jnp.where(mask, jax.nn.log_sigmoid(-z), 0.0)
    L_after = jnp.concatenate([L[..., 1:], jnp.zeros_like(L[..., :1])], axis=-1)
    R = lax.cumsum(L_after, axis=L.ndim - 1, reverse=True) + acc[..., None]
    A = jnp.where(mask, jnp.exp(jax.nn.log_sigmoid(z) + R), 0.0)
    return A, L


def _sb_prompt(q, k, v, bias):
    nb, s, h, dh = q.shape
    f32 = jnp.float32
    qh = jnp.swapaxes(q.astype(f32) * (dh ** -0.5), 1, 2)
    kh = jnp.swapaxes(k.astype(f32), 1, 2)
    vh = jnp.swapaxes(v.astype(f32), 1, 2)
    b4 = bias.astype(f32)[None, :, None, None]
    kpos = jnp.arange(s)

    def block(i):
        start = i * Q_BLOCK
        qb = lax.dynamic_slice_in_dim(qh, start, Q_BLOCK, axis=2)
        z = jnp.einsum('bhqd,bhkd->bhqk', qb, kh) + b4
        mask = kpos[None, :] < (start + jnp.arange(Q_BLOCK))[:, None]
        A, _ = _sb_weights(z, mask, jnp.zeros(z.shape[:-1], f32))
        return jnp.einsum('bhqk,bhkd->bqhd', A, vh)

    out = lax.map(block, jnp.arange(s // Q_BLOCK))
    return jnp.moveaxis(out, 0, 1).reshape(nb, s, h, dh).astype(q.dtype)


def _sb_sample(q, k, v, bias, cache_k, cache_v, page_table, layer):
    nb, t, h, dh = q.shape
    f32 = jnp.float32
    qf = q.astype(f32) * (dh ** -0.5)
    b4 = bias.astype(f32)[None, :, None, None]
    z = jnp.einsum('bthd,bshd->bhts', qf, k.astype(f32)) + b4
    pos = jnp.arange(t)
    A, L = _sb_weights(z, pos[None, :] < pos[:, None], jnp.zeros((nb, h, t), f32))
    acc = L.sum(-1)
    out = jnp.einsum('bhts,bshd->bthd', A, v.astype(f32))
    page_mask = jnp.ones((t, cache_k.shape[2]), bool)

    def step(carry, phys):
        acc, out = carry
        kp = cache_k[layer, phys].astype(f32)
        vp = cache_v[layer, phys].astype(f32)
        zp = jnp.einsum('bthd,bshd->bhts', qf, kp) + b4
        Ap, Lp = _sb_weights(zp, page_mask, acc)
        return (acc + Lp.sum(-1), out + jnp.einsum('bhts,bshd->bthd', Ap, vp)), None

    (_, out), _ = lax.scan(step, (acc, out), page_table.T, reverse=True)
    return out.astype(q.dtype)


def _gated_delta_chunked(q, k, v, g, beta, s0, chunk):
    B, T, H, DK = q.shape
    DV = v.shape[-1]
    nc = T // chunk
    f32 = jnp.float32

    def to_chunks(a):
        a = a.astype(f32).reshape((B, nc, chunk, H) + a.shape[3:])
        return jnp.moveaxis(a, 3, 1)

    q = to_chunks(q) * (DK ** -0.5)
    k, v, beta, g = to_chunks(k), to_chunks(v), to_chunks(beta), to_chunks(g)
    G = jnp.cumsum(g, axis=-1)
    idx = jnp.arange(chunk)
    incl = idx[:, None] >= idx[None, :]
    strict = idx[:, None] > idx[None, :]
    diff = G[..., :, None] - G[..., None, :]
    gam = jnp.where(incl, jnp.exp(jnp.where(incl, diff, 0.0)), 0.0)
    kb = k * beta[..., None]
    a_mat = jnp.where(strict, jnp.einsum('bhncd,bhnsd->bhncs', kb, k) * gam, 0.0) + jnp.eye(chunk, dtype=f32)
    rhs = jnp.concatenate([v * beta[..., None], kb * jnp.exp(G)[..., None]], axis=-1)
    sol = lax.linalg.triangular_solve(a_mat, rhs, left_side=True, lower=True, unit_diagonal=True)
    u, w = sol[..., :DV], sol[..., DV:]
    intra = jnp.where(incl, jnp.einsum('bhncd,bhnsd->bhncs', q, k) * gam, 0.0)
    q_dec = q * jnp.exp(G)[..., None]
    k_dec = k * jnp.exp(G[..., -1:] - G)[..., None]
    g_last = jnp.exp(G[..., -1])

    def step(s, xs):
        u_c, w_c, intra_c, qd_c, kd_c, gl_c = xs
        v_new = u_c - jnp.einsum('bhck,bhkv->bhcv', w_c, s)
        o_c = jnp.einsum('bhck,bhkv->bhcv', qd_c, s) + jnp.einsum('bhcs,bhsv->bhcv', intra_c, v_new)
        s = s * gl_c[..., None, None] + jnp.einsum('bhck,bhcv->bhkv', kd_c, v_new)
        return s, o_c

    xs = tuple(jnp.moveaxis(a, 2, 0) for a in (u, w, intra, q_dec, k_dec, g_last))
    s_final, o = lax.scan(step, s0.astype(f32), xs)
    o = jnp.moveaxis(jnp.moveaxis(o, 0, 2), 1, 3).reshape(B, T, H, DV)
    return o, s_final


def _layer(x, sb_mix, conv_prev, s_prev, chunk, w_in, sb_bias, conv_w, a_log, dt_bias, gdn_norm_w,
           w_o_a, w_o_b, w_out, ln1_g, ln1_b, w_gu, w_down, ln2_g, ln2_b):
    f32 = jnp.float32
    nb, t = x.shape[0], x.shape[1]
    qa, ka, va, conv_in, z_gate, b_raw, a_raw, gate_a, gate_b = _split_columns(x @ w_in)
    heads = lambda a, h: a.reshape(nb, t, h, -1)
    qa, ka, va = heads(qa, SB_HEADS), heads(ka, SB_HEADS), heads(va, SB_HEADS)
    o_a = sb_mix(qa, ka, va, sb_bias).reshape(nb, t, SB_WIDTH)
    conv_ext = jnp.concatenate([conv_prev.astype(conv_in.dtype), conv_in], axis=1)
    c = jax.nn.silu(_causal_conv(conv_ext, conv_w))
    qb = _l2norm(heads(c[..., :GDN_QK_WIDTH], GDN_HEADS))
    kb = _l2norm(heads(c[..., GDN_QK_WIDTH:2 * GDN_QK_WIDTH], GDN_HEADS))
    vb = heads(c[..., 2 * GDN_QK_WIDTH:], GDN_HEADS)
    beta = jax.nn.sigmoid(b_raw.astype(f32))
    g = -jnp.exp(a_log.astype(f32)) * jax.nn.softplus(a_raw.astype(f32) + dt_bias)
    o_b, s_new = _gated_delta_chunked(qb, kb, vb, g, beta, s_prev, chunk)
    o_b = _rmsnorm(o_b) * gdn_norm_w * jax.nn.silu(heads(z_gate, GDN_HEADS).astype(f32))
    o_b = o_b.reshape(nb, t, GDN_V_WIDTH).astype(x.dtype)
    merged = jax.nn.sigmoid(gate_a) * (o_a @ w_o_a) + jax.nn.sigmoid(gate_b) * (o_b @ w_o_b)
    h = _layernorm(DEEPNORM_ALPHA * x + merged @ w_out, ln1_g, ln1_b)
    gu = h @ w_gu
    ffn = (jax.nn.silu(gu[..., :D_FF]) * gu[..., D_FF:]) @ w_down
    y = _layernorm(DEEPNORM_ALPHA * h + ffn, ln2_g, ln2_b)
    return y, (ka, va, s_new.astype(x.dtype), conv_ext[:, -(CONV_WIDTH - 1):])


def setup_inputs(seed: int = 0) -> dict:
    key = jax.random.key(seed)
    ks = jax.random.split(key, 24)
    f32 = jnp.float32
    n_pages = PAST_LEN // PAGE_SIZE
    n_used = DEC_BATCH * n_pages
    n_phys = n_used + max(1, n_used // 4)
    nrm = lambda k, shape, s=1.0: jax.random.normal(k, shape, f32) * s
    x_prompt = nrm(ks[0], (BATCH, SEQ, D_MODEL))
    x_sample = nrm(ks[1], (DEC_BATCH, DEC_SEQ, D_MODEL))
    cache_k = nrm(ks[2], (DEPTH, n_phys, PAGE_SIZE, SB_HEADS, SB_HEAD_DIM))
    cache_v = nrm(ks[3], (DEPTH, n_phys, PAGE_SIZE, SB_HEADS, SB_HEAD_DIM))
    state_gdn = nrm(ks[4], (DEPTH, DEC_BATCH, GDN_HEADS, GDN_KEY_DIM, GDN_VAL_DIM), GDN_KEY_DIM ** -0.5)
    state_conv = nrm(ks[5], (DEPTH, DEC_BATCH, CONV_WIDTH - 1, CONV_CH))
    page_table = jax.random.permutation(ks[6], n_phys)[:n_used].reshape(DEC_BATCH, n_pages).astype(jnp.int32)
    w_in = nrm(ks[7], (DEPTH, D_MODEL, N_IN), D_MODEL ** -0.5)
    sb_bias = jnp.linspace(SB_BIAS_MAX, SB_BIAS_MIN, SB_HEADS, dtype=f32)[None, :] + nrm(ks[21], (DEPTH, SB_HEADS), 0.1)
    conv_w = nrm(ks[8], (DEPTH, CONV_WIDTH, CONV_CH), CONV_WIDTH ** -0.5)
    a_log = jnp.log(jax.random.uniform(ks[9], (DEPTH, GDN_HEADS), f32, 1.0, 16.0))
    dt = jnp.exp(jax.random.uniform(ks[10], (DEPTH, GDN_HEADS), f32, math.log(1e-3), math.log(1e-1)))
    dt_bias = dt + jnp.log(-jnp.expm1(-dt))
    gdn_norm_w = 1.0 + nrm(ks[11], (DEPTH, GDN_VAL_DIM), 0.02)
    w_o_a = nrm(ks[12], (DEPTH, SB_WIDTH, D_MODEL), SB_WIDTH ** -0.5 * DEEPNORM_BETA)
    w_o_b = nrm(ks[13], (DEPTH, GDN_V_WIDTH, D_MODEL), GDN_V_WIDTH ** -0.5 * DEEPNORM_BETA)
    w_out = nrm(ks[14], (DEPTH, D_MODEL, D_MODEL), D_MODEL ** -0.5 * DEEPNORM_BETA)
    ln1_g = 1.0 + nrm(ks[15], (DEPTH, D_MODEL), 0.02)
    ln1_b = nrm(ks[16], (DEPTH, D_MODEL), 0.02)
    w_gu = nrm(ks[17], (DEPTH, D_MODEL, 2 * D_FF), D_MODEL ** -0.5)
    w_down = nrm(ks[18], (DEPTH, D_FF, D_MODEL), D_FF ** -0.5 * DEEPNORM_BETA)
    ln2_g = 1.0 + nrm(ks[19], (DEPTH, D_MODEL), 0.02)
    ln2_b = nrm(ks[20], (DEPTH, D_MODEL), 0.02)
    return {"x_prompt": x_prompt, "x_sample": x_sample, "cache_k": cache_k, "cache_v": cache_v,
            "state_gdn": state_gdn, "state_conv": state_conv, "page_table": page_table,
            "w_in": w_in, "sb_bias": sb_bias, "conv_w": conv_w, "a_log": a_log, "dt_bias": dt_bias,
            "gdn_norm_w": gdn_norm_w, "w_o_a": w_o_a, "w_o_b": w_o_b, "w_out": w_out,
            "ln1_g": ln1_g, "ln1_b": ln1_b, "w_gu": w_gu, "w_down": w_down, "ln2_g": ln2_g, "ln2_b": ln2_b}


def reference(x_prompt, x_sample, cache_k, cache_v, state_gdn, state_conv, page_table,
              w_in, sb_bias, conv_w, a_log, dt_bias, gdn_norm_w, w_o_a, w_o_b, w_out,
              ln1_g, ln1_b, w_gu, w_down, ln2_g, ln2_b):
    yp, ys = x_prompt, x_sample
    kp_l, vp_l, sp_l, cp_l, ks_l, vs_l, ss_l, cs_l = [], [], [], [], [], [], [], []
    for l in range(DEPTH):
        weights = (w_in[l], sb_bias[l], conv_w[l], a_log[l], dt_bias[l], gdn_norm_w[l], w_o_a[l], w_o_b[l],
                   w_out[l], ln1_g[l], ln1_b[l], w_gu[l], w_down[l], ln2_g[l], ln2_b[l])
        nbp = yp.shape[0]
        conv0 = jnp.zeros((nbp, CONV_WIDTH - 1, CONV_CH), yp.dtype)
        s0 = jnp.zeros((nbp, GDN_HEADS, GDN_KEY_DIM, GDN_VAL_DIM), jnp.float32)
        yp, (kp, vp, sp, cp) = _layer(yp, _sb_prompt, conv0, s0, GDN_CHUNK, *weights)
        sb_s = lambda q, k, v, b, layer=l: _sb_sample(q, k, v, b, cache_k, cache_v, page_table, layer)
        ys, (ksm, vsm, ssm, csm) = _layer(ys, sb_s, state_conv[l], state_gdn[l], ys.shape[1], *weights)
        kp_l.append(kp); vp_l.append(vp); sp_l.append(sp); cp_l.append(cp)
        ks_l.append(ksm); vs_l.append(vsm); ss_l.append(ssm); cs_l.append(csm)
    return (yp, ys, jnp.stack(kp_l), jnp.stack(vp_l), jnp.stack(sp_l), jnp.stack(cp_l),
            jnp.stack(ks_l), jnp.stack(vs_l), jnp.stack(ss_l), jnp.stack(cs_l))
```

```python
import functools
import math

import jax
import jax.numpy as jnp
from jax import lax
from jax.experimental import pallas as pl
from jax.experimental.pallas import tpu as pltpu

F32 = jnp.float32
BF16 = jnp.bfloat16

LANES = 128
VMEM_LIMIT = 56 * 1024 * 1024

SB_HEADS = 8
SB_HEAD_DIM = 128
GDN_HEADS = 8
GDN_DIM = 128
CONV_WIDTH = 4
GDN_CHUNK = 64
LN_EPS = 1e-5
RMS_EPS = 1e-6
L2_EPS = 1e-6

SB_TQ = 256
SB_TK = 256


def _params(*sem):
    return pltpu.CompilerParams(dimension_semantics=sem, vmem_limit_bytes=VMEM_LIMIT)


def _dot(a, b):
    return jnp.dot(a, b, preferred_element_type=F32)


def _dot_nt(a, b):
    return lax.dot_general(a, b, (((1,), (1,)), ((), ())), preferred_element_type=F32)


def _dot_tn(a, b):
    return lax.dot_general(a, b, (((0,), (0,)), ((), ())), preferred_element_type=F32)


def _split2(x):
    hi = x.astype(BF16)
    lo = (x - hi.astype(F32)).astype(BF16)
    return hi, lo


def _split3(x):
    hi = x.astype(BF16)
    r = x - hi.astype(F32)
    mid = r.astype(BF16)
    lo = (r - mid.astype(F32)).astype(BF16)
    return hi, mid, lo


def _dot_hp(a, b, dot=_dot):
    a1, a2 = _split2(a)
    b1, b2 = _split2(b)
    return dot(a1, b1) + (dot(a1, b2) + dot(a2, b1))


def _dot_sel(m01, x):
    x1, x2, x3 = _split3(x)
    return _dot(m01, x1) + (_dot(m01, x2) + _dot(m01, x3))


def _neg_softplus(z):
    return jnp.minimum(-z, 0.0) - jnp.log1p(jnp.exp(-jnp.abs(z)))


def _softplus(z):
    return jnp.maximum(z, 0.0) + jnp.log1p(jnp.exp(-jnp.abs(z)))


def _layernorm(x, g, b):
    mu = jnp.mean(x, axis=-1, keepdims=True)
    xc = x - mu
    var = jnp.mean(xc * xc, axis=-1, keepdims=True)
    return xc * lax.rsqrt(var + LN_EPS) * g + b


def _mm_kernel(x_ref, w_ref, o_ref):
    o_ref[...] = _dot(x_ref[...], w_ref[...])


def _matmul(x, w, bm, bn, name):
    m, k = x.shape
    n = w.shape[1]
    bm, bn = min(bm, m), min(bn, n)
    return pl.pallas_call(
        _mm_kernel,
        grid=(n // bn, m // bm),
        in_specs=[pl.BlockSpec((bm, k), lambda j, i: (i, 0)),
                  pl.BlockSpec((k, bn), lambda j, i: (0, j))],
        out_specs=pl.BlockSpec((bm, bn), lambda j, i: (i, j)),
        out_shape=jax.ShapeDtypeStruct((m, n), F32),
        compiler_params=_params("parallel", "parallel"),
        name=name,
    )(x, w)


def _gdn_pre_kernel(cur_ref, halo_ref, prev_ref, cw_ref, ba_ref, gp_ref, q_ref, k_ref, v_ref, beta_ref, g_ref):
    j = pl.program_id(1)
    cur = cur_ref[...]
    halo = jnp.where(j == 0, prev_ref[...], halo_ref[...])
    ext = jnp.concatenate([halo, cur], axis=0)
    cw = cw_ref[...]
    conv = cur * cw[CONV_WIDTH - 1:CONV_WIDTH]
    for d in range(1, CONV_WIDTH):
        conv = conv + pltpu.roll(ext, d, 0)[8:] * cw[CONV_WIDTH - 1 - d:CONV_WIDTH - d]
    c = conv * jax.nn.sigmoid(conv)
    w = GDN_HEADS * GDN_DIM
    for h in range(GDN_HEADS):
        sl = slice(h * GDN_DIM, (h + 1) * GDN_DIM)
        qh = c[:, sl]
        q_ref[:, sl] = qh * lax.rsqrt(jnp.sum(qh * qh, axis=-1, keepdims=True) + L2_EPS) * (GDN_DIM ** -0.5)
        kh = c[:, w + h * GDN_DIM:w + (h + 1) * GDN_DIM]
        k_ref[:, sl] = kh * lax.rsqrt(jnp.sum(kh * kh, axis=-1, keepdims=True) + L2_EPS)
    v_ref[...] = c[:, 2 * w:]
    ba = ba_ref[...]
    gp = gp_ref[...]
    beta = jax.nn.sigmoid(ba)
    g = gp[0:1] * _softplus(ba + gp[1:2])
    for h in range(GDN_HEADS):
        sl = slice(h * GDN_DIM, (h + 1) * GDN_DIM)
        beta_ref[:, sl] = jnp.broadcast_to(beta[:, h:h + 1], (beta.shape[0], GDN_DIM))
        g_ref[:, sl] = jnp.broadcast_to(g[:, GDN_HEADS + h:GDN_HEADS + h + 1], (g.shape[0], GDN_DIM))


def _gdn_pre(conv_in, conv_prev8, conv_w, ba, gparams, bm, name):
    nb, t, c3 = conv_in.shape
    w = c3 // 3
    bm = min(bm, t)
    hb = bm // 8
    row = lambda b, j: (b, j, 0)
    out = jax.ShapeDtypeStruct((nb, t, w), F32)
    return pl.pallas_call(
        _gdn_pre_kernel,
        grid=(nb, t // bm),
        in_specs=[pl.BlockSpec((None, bm, c3), row),
                  pl.BlockSpec((None, 8, c3), lambda b, j: (b, jnp.maximum(j * hb - 1, 0), 0)),
                  pl.BlockSpec((None, 8, c3), lambda b, j: (b, 0, 0)),
                  pl.BlockSpec((CONV_WIDTH, c3), lambda b, j: (0, 0)),
                  pl.BlockSpec((None, bm, LANES), row),
                  pl.BlockSpec((8, LANES), lambda b, j: (0, 0))],
        out_specs=[pl.BlockSpec((None, bm, w), row)] * 5,
        out_shape=[out] * 5,
        compiler_params=_params("parallel", "parallel"),
        name=name,
    )(conv_in, conv_in, conv_prev8, conv_w, ba, gparams)


def _gdn_kernel(q_ref, k_ref, v_ref, beta_ref, g_ref, s0_ref, o_ref, sfin_ref, s_scr, *, heads):
    c = pl.program_id(2)
    nc = pl.num_programs(2)
    n = q_ref.shape[0]

    @pl.when(c == 0)
    def _():
        s_scr[...] = s0_ref[...]

    ri = lax.broadcasted_iota(jnp.int32, (n, n), 0)
    ci = lax.broadcasted_iota(jnp.int32, (n, n), 1)
    incl = ri >= ci
    strict = ri > ci
    eye = (ri == ci).astype(F32)
    tri = incl.astype(BF16)
    ones = jnp.ones((n, n), BF16)

    for h in range(heads):
        sl = slice(h * GDN_DIM, (h + 1) * GDN_DIM)
        q, k, v = q_ref[:, sl], k_ref[:, sl], v_ref[:, sl]
        beta, g = beta_ref[:, sl], g_ref[:, sl]
        gcum = _dot_sel(tri, g)
        grow = _dot_sel(ones, gcum[:, :n] * eye)
        diff = gcum[:, :n] - grow
        gam = jnp.where(incl, jnp.exp(jnp.where(incl, diff, 0.0)), 0.0)
        eg = jnp.exp(gcum)
        glast = gcum[n - 1:n, :]
        kb = k * beta
        m = -jnp.where(strict, _dot_hp(kb, k, _dot_nt) * gam, 0.0)
        rhs = jnp.concatenate([v * beta, kb * eg], axis=-1)
        pw = [m]
        while (1 << len(pw)) < n:
            pw.append(_dot_hp(pw[-1], pw[-1]))
        sol = rhs
        for mp in reversed(pw):
            sol = sol + _dot_hp(mp, sol)
        u, w = sol[:, :GDN_DIM], sol[:, GDN_DIM:]
        intra = jnp.where(incl, _dot_hp(q, k, _dot_nt) * gam, 0.0)
        s = s_scr[h]
        v_new = u - _dot_hp(w, s)
        o_ref[:, sl] = _dot_hp(q * eg, s) + _dot_hp(intra, v_new)
        kd = k * jnp.exp(glast - gcum)
        s_new = s * jnp.exp(glast)[0:1, :] + _dot_hp(kd.T, v_new)
        s_scr[h] = s_new

    @pl.when(c == nc - 1)
    def _():
        sfin_ref[...] = s_scr[...]


def _gdn(q, k, v, beta, g, s0, heads_per_step, name):
    nb, t, w = q.shape
    hg = heads_per_step
    cw = hg * GDN_DIM
    tok = pl.BlockSpec((None, GDN_CHUNK, cw), lambda b, hh, c: (b, c, hh))
    st = pl.BlockSpec((None, hg, GDN_DIM, GDN_DIM), lambda b, hh, c: (b, hh, 0, 0))
    return pl.pallas_call(
        functools.partial(_gdn_kernel, heads=hg),
        grid=(nb, GDN_HEADS // hg, t // GDN_CHUNK),
        in_specs=[tok] * 5 + [st],
        out_specs=[tok, st],
        out_shape=[jax.ShapeDtypeStruct((nb, t, w), F32),
                   jax.ShapeDtypeStruct((nb, GDN_HEADS, GDN_DIM, GDN_DIM), F32)],
        scratch_shapes=[pltpu.VMEM((hg, GDN_DIM, GDN_DIM), F32)],
        compiler_params=_params("parallel", "parallel", "arbitrary"),
        name=name,
    )(q, k, v, beta, g, s0)


def _sbp_kernel(bias_ref, qt_ref, k_ref, vt_ref, u2_ref, o_ref):
    h = pl.program_id(0)
    i = pl.program_id(1)
    bias = bias_ref[h]
    qt = qt_ref[...]
    u2 = u2_ref[...]
    tk, tq = SB_TK, SB_TQ

    def block(j, acc, out, masked):
        kb = k_ref[pl.ds(pl.multiple_of(j * tk, tk), tk), :]
        z = _dot(kb, qt) + bias
        lg = _neg_softplus(z)
        if masked:
            valid = lax.broadcasted_iota(jnp.int32, (tk, tq), 0) < lax.broadcasted_iota(jnp.int32, (tk, tq), 1)
            lg = jnp.where(valid, lg, 0.0)
        hi, lo = _split2(lg)
        rinc = _dot(u2, jnp.concatenate([hi, lo], axis=0))
        a = jnp.exp(z + rinc + acc)
        if masked:
            a = jnp.where(valid, a, 0.0)
        out = out + _dot(vt_ref[j], a.astype(BF16))
        return acc + rinc[0:1, :], out

    acc, out = block(i, jnp.zeros((1, tq), F32), jnp.zeros((SB_HEAD_DIM, tq), F32), True)

    def body(n, carry):
        return block(i - 1 - n, carry[0], carry[1], False)

    acc, out = lax.fori_loop(0, i, body, (acc, out))
    o_ref[...] = out


def _sb_prompt(q, k, v, bias):
    s = q.shape[0]
    hh, dh = SB_HEADS, SB_HEAD_DIM
    nkb = s // SB_TK
    qt = (q * (dh ** -0.5)).astype(BF16).reshape(s, hh, dh).transpose(1, 2, 0)
    kb = k.astype(BF16)
    vt = v.astype(BF16).reshape(nkb, SB_TK, hh, dh).transpose(2, 0, 3, 1)
    r = jnp.arange(SB_TK)
    u = (r[None, :] >= r[:, None]).astype(BF16)
    u2 = jnp.concatenate([u, u], axis=1)
    out_t = pl.pallas_call(
        _sbp_kernel,
        grid=(hh, s // SB_TQ),
        in_specs=[pl.BlockSpec(memory_space=pltpu.SMEM),
                  pl.BlockSpec((None, dh, SB_TQ), lambda h, i: (h, 0, i)),
                  pl.BlockSpec((s, dh), lambda h, i: (0, h)),
                  pl.BlockSpec((None, nkb, dh, SB_TK), lambda h, i: (h, 0, 0, 0)),
                  pl.BlockSpec((SB_TK, 2 * SB_TK), lambda h, i: (0, 0))],
        out_specs=pl.BlockSpec((None, dh, SB_TQ), lambda h, i: (h, 0, i)),
        out_shape=jax.ShapeDtypeStruct((hh, dh, s), F32),
        compiler_params=_params("parallel", "arbitrary"),
        name="sb_prompt",
    )(bias.astype(F32), qt, kb, vt, u2)
    return out_t.transpose(2, 0, 1).reshape(s, hh * dh)


def _sbs_kernel(pt_ref, qbd_ref, bias_ref, knew_ref, vnew_ref, kc_ref, vc_ref, u2_ref, o_ref, acc_scr, out_scr):
    p = pl.program_id(1)
    qbd = qbd_ref[...]
    bias = bias_ref[...]
    u2 = u2_ref[...]
    nrow, npos = bias.shape
    t = nrow // SB_HEADS

    def page(kp, vp, masked):
        z = _dot_nt(qbd, kp.astype(BF16)) + bias
        lg = _neg_softplus(z)
        if masked:
            tok = lax.broadcasted_iota(jnp.int32, (nrow, npos), 0) % t
            valid = lax.broadcasted_iota(jnp.int32, (nrow, npos), 1) < tok
            lg = jnp.where(valid, lg, 0.0)
        hi, lo = _split2(lg)
        rinc = _dot(jnp.concatenate([hi, lo], axis=1), u2)
        a = jnp.exp(z + rinc + acc_scr[...])
        if masked:
            a = jnp.where(valid, a, 0.0)
        out_scr[...] += _dot(a.astype(BF16), vp.astype(BF16))
        acc_scr[...] += jnp.sum(lg, axis=-1, keepdims=True)

    @pl.when(p == 0)
    def _():
        acc_scr[...] = jnp.zeros_like(acc_scr)
        out_scr[...] = jnp.zeros_like(out_scr)
        page(knew_ref[...], vnew_ref[...], True)

    page(kc_ref[...], vc_ref[...], False)

    @pl.when(p == pl.num_programs(1) - 1)
    def _():
        for h in range(SB_HEADS):
            sl = slice(h * SB_HEAD_DIM, (h + 1) * SB_HEAD_DIM)
            o_ref[:, sl] = out_scr[h * t:(h + 1) * t, sl]


def _sb_sample(q, k, v, bias, cache_k, cache_v, page_table):
    nb, t, w = q.shape
    hh, dh = SB_HEADS, SB_HEAD_DIM
    npg = page_table.shape[1]
    psz = cache_k.shape[1]
    qs = (q * (dh ** -0.5)).astype(BF16).reshape(nb, t, hh, dh)
    eye = jnp.eye(hh, dtype=BF16)
    qbd = (qs.transpose(0, 2, 1, 3)[:, :, :, None, :] * eye[None, :, None, :, None]).reshape(nb, hh * t, w)
    bias_rep = jnp.broadcast_to(jnp.repeat(bias.astype(F32), t)[:, None], (hh * t, psz))
    pad = ((0, 0), (0, psz - t), (0, 0))
    knew, vnew = jnp.pad(k, pad), jnp.pad(v, pad)
    r = jnp.arange(psz)
    u = (r[:, None] >= r[None, :]).astype(BF16)
    u2 = jnp.concatenate([u, u], axis=0)
    per_seq = lambda b, p, pt: (b, 0, 0)
    cache = pl.BlockSpec((None, psz, w), lambda b, p, pt: (pt[b, npg - 1 - p], 0, 0))
    return pl.pallas_call(
        _sbs_kernel,
        grid_spec=pltpu.PrefetchScalarGridSpec(
            num_scalar_prefetch=1,
            grid=(nb, npg),
            in_specs=[pl.BlockSpec((None, hh * t, w), per_seq),
                      pl.BlockSpec((hh * t, psz), lambda b, p, pt: (0, 0)),
                      pl.BlockSpec((None, psz, w), per_seq),
                      pl.BlockSpec((None, psz, w), per_seq),
                      cache, cache,
                      pl.BlockSpec((2 * psz, psz), lambda b, p, pt: (0, 0))],
            out_specs=pl.BlockSpec((None, t, w), per_seq),
            scratch_shapes=[pltpu.VMEM((hh * t, psz), F32), pltpu.VMEM((hh * t, w), F32)],
        ),
        out_shape=jax.ShapeDtypeStruct((nb, t, w), F32),
        compiler_params=_params("parallel", "arbitrary"),
        name="sb_sample",
    )(page_table, qbd, bias_rep, knew, vnew, cache_k, cache_v, u2)


def _merge_kernel(oa_ref, ob_ref, zg_ref, ga_ref, gb_ref, x_ref, nw_ref, woa_ref, wob_ref, wout_ref,
                  g1_ref, b1_ref, h_ref, hb_ref, *, alpha):
    ob = ob_ref[...]
    zg = zg_ref[...]
    nw = nw_ref[...]
    parts = []
    for h in range(GDN_HEADS):
        sl = slice(h * GDN_DIM, (h + 1) * GDN_DIM)
        oh = ob[:, sl]
        zh = zg[:, sl]
        rn = oh * lax.rsqrt(jnp.mean(oh * oh, axis=-1, keepdims=True) + RMS_EPS)
        parts.append((rn * nw * (zh * jax.nn.sigmoid(zh))).astype(BF16))
    obn = jnp.concatenate(parts, axis=-1)
    merged = (jax.nn.sigmoid(ga_ref[...]) * _dot(oa_ref[...], woa_ref[...])
              + jax.nn.sigmoid(gb_ref[...]) * _dot(obn, wob_ref[...]))
    y = alpha * x_ref[...] + _dot(merged.astype(BF16), wout_ref[...])
    hn = _layernorm(y, g1_ref[...], b1_ref[...])
    h_ref[...] = hn
    hb_ref[...] = hn.astype(BF16)


def _merge(oa, ob, zg, ga, gb, x, nw, woa, wob, wout, g1, b1, alpha, bm, name):
    m, d = x.shape
    w = oa.shape[1]
    bm = min(bm, m)
    row = lambda i: (i, 0)
    fix = lambda i: (0, 0)
    once = functools.partial(pl.BlockSpec, index_map=fix, pipeline_mode=pl.Buffered(1))
    return pl.pallas_call(
        functools.partial(_merge_kernel, alpha=alpha),
        grid=(m // bm,),
        in_specs=[pl.BlockSpec((bm, w), row), pl.BlockSpec((bm, w), row), pl.BlockSpec((bm, w), row),
                  pl.BlockSpec((bm, d), row), pl.BlockSpec((bm, d), row), pl.BlockSpec((bm, d), row),
                  pl.BlockSpec((1, GDN_DIM), fix),
                  once((w, d)), once((w, d)), once((d, d)),
                  pl.BlockSpec((1, d), fix), pl.BlockSpec((1, d), fix)],
        out_specs=[pl.BlockSpec((bm, d), row), pl.BlockSpec((bm, d), row)],
        out_shape=[jax.ShapeDtypeStruct((m, d), F32), jax.ShapeDtypeStruct((m, d), BF16)],
        compiler_params=_params("parallel"),
        name=name,
    )(oa, ob, zg, ga, gb, x, nw, woa, wob, wout, g1, b1)


def _ffn_kernel(hb_ref, h_ref, wg_ref, wu_ref, wd_ref, g2_ref, b2_ref, y_ref, acc_ref, *, alpha):
    f = pl.program_id(1)
    hb = hb_ref[...]
    gate = _dot(hb, wg_ref[...])
    up = _dot(hb, wu_ref[...])
    act = (gate * jax.nn.sigmoid(gate) * up).astype(BF16)
    part = _dot(act, wd_ref[...])

    @pl.when(f == 0)
    def _():
        acc_ref[...] = part

    @pl.when(f != 0)
    def _():
        acc_ref[...] += part

    @pl.when(f == pl.num_programs(1) - 1)
    def _():
        y_ref[...] = _layernorm(alpha * h_ref[...] + acc_ref[...], g2_ref[...], b2_ref[...])


def _ffn(hb, h, wg, wu, wd, g2, b2, alpha, bm, bf, name):
    m, d = h.shape
    dff = wg.shape[1]
    bm = min(bm, m)
    row = lambda i, f: (i, 0)
    fix = lambda i, f: (0, 0)
    return pl.pallas_call(
        functools.partial(_ffn_kernel, alpha=alpha),
        grid=(m // bm, dff // bf),
        in_specs=[pl.BlockSpec((bm, d), row), pl.BlockSpec((bm, d), row),
                  pl.BlockSpec((d, bf), lambda i, f: (0, f)), pl.BlockSpec((d, bf), lambda i, f: (0, f)),
                  pl.BlockSpec((bf, d), lambda i, f: (f, 0)),
                  pl.BlockSpec((1, d), fix), pl.BlockSpec((1, d), fix)],
        out_specs=pl.BlockSpec((bm, d), row),
        out_shape=jax.ShapeDtypeStruct((m, d), F32),
        scratch_shapes=[pltpu.VMEM((bm, d), F32)],
        compiler_params=_params("parallel", "arbitrary"),
        name=name,
    )(hb, h, wg, wu, wd, g2, b2)


def _layer(x, sb_mix, conv_prev, s_prev, wts, alpha, tag, bm):
    nb, t, d = x.shape
    m = nb * t
    w = SB_HEADS * SB_HEAD_DIM
    xf = x.reshape(m, d)
    xb = xf.astype(BF16)
    mm = lambda wt, nm: _matmul(xb, wt, bm, 1024, f"proj_{nm}_{tag}")
    qa, ka, va = mm(wts["w_q"], "q"), mm(wts["w_k"], "k"), mm(wts["w_v"], "v")
    conv_in = mm(wts["w_conv"], "conv").reshape(nb, t, 3 * w)
    zg, ga, gb, ba = mm(wts["w_z"], "z"), mm(wts["w_ga"], "ga"), mm(wts["w_gb"], "gb"), mm(wts["w_ba"], "ba")

    o_a = sb_mix(qa, ka, va)

    prev8 = jnp.pad(conv_prev.astype(F32), ((0, 0), (8 - (CONV_WIDTH - 1), 0), (0, 0)))
    qn, kn, vv, beta, g = _gdn_pre(conv_in, prev8, wts["conv_w"], ba.reshape(nb, t, LANES), wts["gparams"],
                                   256, f"gdn_pre_{tag}")
    tp = -(-t // GDN_CHUNK) * GDN_CHUNK
    if tp != t:
        padt = lambda a: jnp.pad(a, ((0, 0), (0, tp - t), (0, 0)))
        qn, kn, vv, beta, g = map(padt, (qn, kn, vv, beta, g))
    o_b, s_new = _gdn(qn, kn, vv, beta, g, s_prev.astype(F32), 2, f"gdn_{tag}")
    o_b = o_b[:, :t].reshape(m, w)

    h, hb = _merge(o_a.astype(BF16), o_b, zg, ga, gb, xf, wts["norm_w"], wts["w_o_a"], wts["w_o_b"], wts["w_out"],
                   wts["ln1_g"], wts["ln1_b"], alpha, 256, f"merge_{tag}")
    y = _ffn(hb, h, wts["w_g"], wts["w_u"], wts["w_down"], wts["ln2_g"], wts["ln2_b"], alpha, 512, 512,
             f"ffn_{tag}")
    conv_new = jnp.concatenate([conv_prev.astype(F32), conv_in], axis=1)[:, -(CONV_WIDTH - 1):]
    return (y.reshape(nb, t, d),
            (ka.reshape(nb, t, SB_HEADS, SB_HEAD_DIM), va.reshape(nb, t, SB_HEADS, SB_HEAD_DIM), s_new, conv_new))


def _prep_weights(w_in, sb_bias, conv_w, a_log, dt_bias, gdn_norm_w, w_o_a, w_o_b, w_out,
                  ln1_g, ln1_b, w_gu, w_down, ln2_g, ln2_b):
    w = SB_HEADS * SB_HEAD_DIM
    d = w_in.shape[0]
    dff = w_down.shape[0]
    hh = GDN_HEADS
    o = 0
    cols = {}
    for nm, width in (("w_q", w), ("w_k", w), ("w_v", w), ("w_conv", 3 * w), ("w_z", w),
                      ("w_ba", 2 * hh), ("w_ga", d), ("w_gb", d)):
        cols[nm] = w_in[:, o:o + width].astype(BF16)
        o += width
    cols["w_ba"] = jnp.pad(cols["w_ba"], ((0, 0), (0, LANES - 2 * hh)))
    gp = jnp.zeros((8, LANES), F32)
    gp = gp.at[0, hh:2 * hh].set(-jnp.exp(a_log.astype(F32))).at[1, hh:2 * hh].set(dt_bias.astype(F32))
    cols.update(
        sb_bias=sb_bias, conv_w=conv_w.astype(F32), gparams=gp,
        norm_w=gdn_norm_w.astype(F32).reshape(1, GDN_DIM),
        w_o_a=w_o_a.astype(BF16), w_o_b=w_o_b.astype(BF16), w_out=w_out.astype(BF16),
        ln1_g=ln1_g.reshape(1, d), ln1_b=ln1_b.reshape(1, d),
        w_g=w_gu[:, :dff].astype(BF16), w_u=w_gu[:, dff:].astype(BF16), w_down=w_down.astype(BF16),
        ln2_g=ln2_g.reshape(1, d), ln2_b=ln2_b.reshape(1, d))
    return cols


def kernel(x_prompt, x_sample, cache_k, cache_v, state_gdn, state_conv, page_table, w_in, sb_bias, conv_w,
           a_log, dt_bias, gdn_norm_w, w_o_a, w_o_b, w_out, ln1_g, ln1_b, w_gu, w_down, ln2_g, ln2_b):
    depth = w_in.shape[0]
    alpha = (2.0 * depth) ** 0.25
    w = SB_HEADS * SB_HEAD_DIM
    yp, ys = x_prompt, x_sample
    outs = [[] for _ in range(8)]
    for l in range(depth):
        wts = _prep_weights(w_in[l], sb_bias[l], conv_w[l], a_log[l], dt_bias[l], gdn_norm_w[l], w_o_a[l],
                            w_o_b[l], w_out[l], ln1_g[l], ln1_b[l], w_gu[l], w_down[l], ln2_g[l], ln2_b[l])
        nbp, tp = yp.shape[0], yp.shape[1]
        conv0 = jnp.zeros((nbp, CONV_WIDTH - 1, 3 * w), F32)
        s0 = jnp.zeros((nbp, GDN_HEADS, GDN_DIM, GDN_DIM), F32)

        def sb_p(q, k, v):
            return jnp.concatenate([_sb_prompt(q[b * tp:(b + 1) * tp], k[b * tp:(b + 1) * tp],
                                               v[b * tp:(b + 1) * tp], wts["sb_bias"]) for b in range(nbp)], axis=0)

        yp, (kp, vp, sp, cp) = _layer(yp, sb_p, conv0, s0, wts, alpha, "p", 1024)

        nbs, ts = ys.shape[0], ys.shape[1]
        ck = cache_k[l].reshape(cache_k.shape[1], cache_k.shape[2], w)
        cv = cache_v[l].reshape(cache_v.shape[1], cache_v.shape[2], w)

        def sb_s(q, k, v):
            r3 = lambda a: a.reshape(nbs, ts, w)
            return _sb_sample(r3(q), r3(k), r3(v), wts["sb_bias"], ck, cv, page_table).reshape(nbs * ts, w)

        ys, (ksm, vsm, ssm, csm) = _layer(ys, sb_s, state_conv[l], state_gdn[l], wts, alpha, "s", 256)
        for lst, val in zip(outs, (kp, vp, sp, cp, ksm, vsm, ssm, csm)):
            lst.append(val)
    return (yp, ys) + tuple(jnp.stack(o) for o in outs)
```

```python
import functools

import jax
import jax.numpy as jnp
from jax import lax
from jax.experimental import pallas as pl
from jax.experimental.pallas import tpu as pltpu

F32 = jnp.float32
BF16 = jnp.bfloat16

LANES = 128
VMEM_LIMIT = 56 * 1024 * 1024

SB_HEADS = 8
SB_HEAD_DIM = 128
GDN_HEADS = 8
GDN_DIM = 128
CONV_WIDTH = 4
GDN_CHUNK = 64
LN_EPS = 1e-5
RMS_EPS = 1e-6
L2_EPS = 1e-6

SB_TQ = 256
SB_TK = 256
SB_HEAD_GROUP = 4
SB_PAGE_GROUP = 8
GDN_CHUNK_GROUP = 8


def _params(*sem):
    return pltpu.CompilerParams(dimension_semantics=sem, vmem_limit_bytes=VMEM_LIMIT)


def _dot(a, b):
    return jnp.dot(a, b, preferred_element_type=F32)


def _dot_nt(a, b):
    return lax.dot_general(a, b, (((1,), (1,)), ((), ())), preferred_element_type=F32)


def _dot_tn(a, b):
    return lax.dot_general(a, b, (((0,), (0,)), ((), ())), preferred_element_type=F32)


def _split2(x):
    hi = x.astype(BF16)
    lo = (x - hi.astype(F32)).astype(BF16)
    return hi, lo


def _split3(x):
    hi = x.astype(BF16)
    r = x - hi.astype(F32)
    mid = r.astype(BF16)
    lo = (r - mid.astype(F32)).astype(BF16)
    return hi, mid, lo


def _dot_sel(m01, x):
    x1, x2, x3 = _split3(x)
    return _dot(m01, x1) + (_dot(m01, x2) + _dot(m01, x3))


def _neg_softplus(z):
    return jnp.minimum(-z, 0.0) - jnp.log(1.0 + jnp.exp(-jnp.abs(z)))


LOG2E = 1.4426950408889634


def _log2_one_minus_sigmoid(nz):
    nabs = pltpu.bitcast(pltpu.bitcast(nz, jnp.uint32) | jnp.uint32(0x80000000), F32)
    return jnp.minimum(nz, 0.0) - jnp.log(1.0 + jnp.exp2(nabs)) * LOG2E


def _softplus(z):
    return jnp.maximum(z, 0.0) + jnp.log1p(jnp.exp(-jnp.abs(z)))


def _layernorm(x, g, b):
    mu = jnp.mean(x, axis=-1, keepdims=True)
    xc = x - mu
    var = jnp.mean(xc * xc, axis=-1, keepdims=True)
    return xc * lax.rsqrt(var + LN_EPS) * g + b


def _mm_kernel(x_ref, w_ref, o_ref):
    o_ref[...] = _dot(x_ref[...], w_ref[...])


def _matmul(x, w, bm, bn, name):
    m, k = x.shape
    n = w.shape[1]
    bm, bn = min(bm, m), min(bn, n)
    return pl.pallas_call(
        _mm_kernel,
        grid=(n // bn, m // bm),
        in_specs=[pl.BlockSpec((bm, k), lambda j, i: (i, 0)),
                  pl.BlockSpec((k, bn), lambda j, i: (0, j))],
        out_specs=pl.BlockSpec((bm, bn), lambda j, i: (i, j)),
        out_shape=jax.ShapeDtypeStruct((m, n), F32),
        compiler_params=_params("parallel", "parallel"),
        name=name,
    )(x, w)


def _gdn_pre_kernel(cur_ref, halo_ref, prev_ref, cw_ref, ba_ref, gp_ref, q_ref, k_ref, v_ref, beta_ref, g_ref):
    j = pl.program_id(1)
    cur = cur_ref[...]
    halo = jnp.where(j == 0, prev_ref[...], halo_ref[...])
    ext = jnp.concatenate([halo, cur], axis=0)
    cw = cw_ref[...]
    conv = cur * cw[CONV_WIDTH - 1:CONV_WIDTH]
    for d in range(1, CONV_WIDTH):
        conv = conv + pltpu.roll(ext, d, 0)[8:] * cw[CONV_WIDTH - 1 - d:CONV_WIDTH - d]
    c = conv * jax.nn.sigmoid(conv)
    w = GDN_HEADS * GDN_DIM
    for h in range(GDN_HEADS):
        sl = slice(h * GDN_DIM, (h + 1) * GDN_DIM)
        qh = c[:, sl]
        q_ref[:, sl] = qh * lax.rsqrt(jnp.sum(qh * qh, axis=-1, keepdims=True) + L2_EPS) * (GDN_DIM ** -0.5)
        kh = c[:, w + h * GDN_DIM:w + (h + 1) * GDN_DIM]
        k_ref[:, sl] = kh * lax.rsqrt(jnp.sum(kh * kh, axis=-1, keepdims=True) + L2_EPS)
    v_ref[...] = c[:, 2 * w:]
    ba = ba_ref[...]
    gp = gp_ref[...]
    beta = jax.nn.sigmoid(ba)
    g = gp[0:1] * _softplus(ba + gp[1:2])
    for h in range(GDN_HEADS):
        sl = slice(h * GDN_DIM, (h + 1) * GDN_DIM)
        beta_ref[:, sl] = jnp.broadcast_to(beta[:, h:h + 1], (beta.shape[0], GDN_DIM))
        g_ref[:, sl] = jnp.broadcast_to(g[:, GDN_HEADS + h:GDN_HEADS + h + 1], (g.shape[0], GDN_DIM))


def _gdn_pre(conv_in, conv_prev8, conv_w, ba, gparams, bm, name):
    nb, t, c3 = conv_in.shape
    w = c3 // 3
    bm = min(bm, t)
    hb = bm // 8
    row = lambda b, j: (b, j, 0)
    out = jax.ShapeDtypeStruct((nb, t, w), F32)
    return pl.pallas_call(
        _gdn_pre_kernel,
        grid=(nb, t // bm),
        in_specs=[pl.BlockSpec((None, bm, c3), row),
                  pl.BlockSpec((None, 8, c3), lambda b, j: (b, jnp.maximum(j * hb - 1, 0), 0)),
                  pl.BlockSpec((None, 8, c3), lambda b, j: (b, 0, 0)),
                  pl.BlockSpec((CONV_WIDTH, c3), lambda b, j: (0, 0)),
                  pl.BlockSpec((None, bm, LANES), row),
                  pl.BlockSpec((8, LANES), lambda b, j: (0, 0))],
        out_specs=[pl.BlockSpec((None, bm, w), row)] * 5,
        out_shape=[out] * 5,
        compiler_params=_params("parallel", "parallel"),
        name=name,
    )(conv_in, conv_in, conv_prev8, conv_w, ba, gparams)


def _gdn_prep_kernel(q_ref, k_ref, v_ref, beta_ref, g_ref, u_ref, w_ref, qd_ref, kd_ref, intra_ref, gl_ref,
                     *, chunks):
    n = GDN_CHUNK
    ri = lax.broadcasted_iota(jnp.int32, (n, n), 0)
    ci = lax.broadcasted_iota(jnp.int32, (n, n), 1)
    incl = ri >= ci
    strict = ri > ci
    tri = incl.astype(BF16)
    upper = (ri <= ci).astype(F32)
    ones = jnp.ones((n, n), BF16)
    lane_pad = jnp.zeros((n, GDN_DIM - n), F32)

    def chunk(c, carry):
        rows = pl.ds(pl.multiple_of(c * n, n), n)
        hs = range(GDN_HEADS)
        sls = [slice(h * GDN_DIM, (h + 1) * GDN_DIM) for h in hs]
        ks = [k_ref[rows, sl] for sl in sls]
        betas = [beta_ref[rows, sl] for sl in sls]
        gs = [g_ref[rows, sl] for sl in sls]
        gcums = [_dot_sel(tri, g) for g in gs]
        grows = [_dot_sel(ones, g[:, :n] * upper) for g in gs]
        kbfs = [k.astype(BF16) for k in ks]
        kbs = [k * beta for k, beta in zip(ks, betas)]
        kks = [_dot_nt(kb.astype(BF16), kbf) for kb, kbf in zip(kbs, kbfs)]
        qks = [_dot_nt(q_ref[rows, sl].astype(BF16), kbf) for sl, kbf in zip(sls, kbfs)]
        gams = [jnp.where(incl, jnp.exp(jnp.where(incl, gc[:, :n] - gr, 0.0)), 0.0) for gc, gr in zip(gcums, grows)]
        egs = [jnp.exp(gc) for gc in gcums]
        mps = [-jnp.where(strict, kk * gam, 0.0) for kk, gam in zip(kks, gams)]
        sols = [jnp.concatenate([v_ref[rows, sl] * beta, kb * eg], axis=-1)
                for sl, beta, kb, eg in zip(sls, betas, kbs, egs)]
        span = 1
        while span < n:
            mpbs = [mp.astype(BF16) for mp in mps]
            upds = [_dot(mpb, jnp.concatenate(_split2(sol), axis=-1)) for mpb, sol in zip(mpbs, sols)]
            sols = [sol + (upd[:, :2 * GDN_DIM] + upd[:, 2 * GDN_DIM:]) for sol, upd in zip(sols, upds)]
            span *= 2
            if span < n:
                mps = [_dot(mpb, mpb) for mpb in mpbs]
        for h in hs:
            sl = sls[h]
            glast = gcums[h][n - 1:n, :]
            u_ref[rows, sl] = sols[h][:, :GDN_DIM]
            w_ref[rows, sl] = sols[h][:, GDN_DIM:].astype(BF16)
            intra = jnp.where(incl, qks[h] * gams[h], 0.0)
            intra_ref[rows, sl] = jnp.concatenate([intra, lane_pad], axis=-1).astype(BF16)
            qd_ref[rows, sl] = (q_ref[rows, sl] * egs[h]).astype(BF16)
            kd_ref[rows, sl] = (ks[h] * jnp.exp(glast - gcums[h])).astype(BF16)
            gl_ref[c, h:h + 1, :] = jnp.exp(glast)
        return carry

    lax.fori_loop(0, chunks, chunk, 0)


def _gdn_scan_kernel(u_ref, w_ref, qd_ref, kd_ref, intra_ref, gl_ref, s0_ref, o_ref, sfin_ref, s_scr, *, chunks):
    n = GDN_CHUNK
    cg = pl.program_id(1)

    @pl.when(cg == 0)
    def _():
        s_scr[...] = s0_ref[...]

    def chunk(c, carry):
        rows = pl.ds(pl.multiple_of(c * n, n), n)
        hs = range(GDN_HEADS)
        sls = [slice(h * GDN_DIM, (h + 1) * GDN_DIM) for h in hs]
        ss = [s_scr[h] for h in hs]
        rs = [_dot(jnp.concatenate([w_ref[rows, sl], qd_ref[rows, sl]], axis=0), s.astype(BF16))
              for sl, s in zip(sls, ss)]
        vbs = [(u_ref[rows, sl] - r[:n]).astype(BF16) for sl, r in zip(sls, rs)]
        upds = [_dot_tn(kd_ref[rows, sl], vb) for sl, vb in zip(sls, vbs)]
        for h in hs:
            s_scr[h] = ss[h] * gl_ref[c, h:h + 1, :] + upds[h]
        for h in hs:
            o_ref[rows, sls[h]] = rs[h][n:] + _dot(intra_ref[rows, h * GDN_DIM:h * GDN_DIM + n], vbs[h])
        return carry

    lax.fori_loop(0, chunks, chunk, 0)

    @pl.when(cg == pl.num_programs(1) - 1)
    def _():
        sfin_ref[...] = s_scr[...]


def _gdn(q, k, v, beta, g, s0, tag):
    nb, t, w = q.shape
    nc = t // GDN_CHUNK
    cgs = min(GDN_CHUNK_GROUP, nc)
    rows = cgs * GDN_CHUNK
    tok = pl.BlockSpec((None, rows, w), lambda b, c: (b, c, 0))
    gl = pl.BlockSpec((None, cgs, GDN_HEADS, GDN_DIM), lambda b, c: (b, c, 0, 0))
    st = pl.BlockSpec((None, GDN_HEADS, GDN_DIM, GDN_DIM), lambda b, c: (b, 0, 0, 0))
    tok_f32 = jax.ShapeDtypeStruct((nb, t, w), F32)
    tok_bf16 = jax.ShapeDtypeStruct((nb, t, w), BF16)
    gl_shape = jax.ShapeDtypeStruct((nb, nc, GDN_HEADS, GDN_DIM), F32)
    u, wy, qd, kd, intra, glast = pl.pallas_call(
        functools.partial(_gdn_prep_kernel, chunks=cgs),
        grid=(nb, nc // cgs),
        in_specs=[tok] * 5,
        out_specs=[tok] * 5 + [gl],
        out_shape=[tok_f32] + [tok_bf16] * 4 + [gl_shape],
        compiler_params=_params("parallel", "parallel"),
        name=f"gdn_prep_{tag}",
    )(q, k, v, beta, g)
    return pl.pallas_call(
        functools.partial(_gdn_scan_kernel, chunks=cgs),
        grid=(nb, nc // cgs),
        in_specs=[tok] * 5 + [gl, st],
        out_specs=[tok, st],
        out_shape=[tok_f32, jax.ShapeDtypeStruct((nb, GDN_HEADS, GDN_DIM, GDN_DIM), F32)],
        scratch_shapes=[pltpu.VMEM((GDN_HEADS, GDN_DIM, GDN_DIM), F32)],
        compiler_params=_params("parallel", "arbitrary"),
        name=f"gdn_scan_{tag}",
    )(u, wy, qd, kd, intra, glast, s0)


def _sbp_kernel(bias_ref, qt_ref, k_ref, vt_ref, u2_ref, o_ref):
    hg = pl.program_id(0)
    i = pl.program_id(1)
    nh = qt_ref.shape[0]
    u2 = u2_ref[...]
    tk, tq = SB_TK, SB_TQ

    def blocks(js, accs, masked):
        hs = range(nh)
        if masked:
            valid = lax.broadcasted_iota(jnp.int32, (tk, tq), 0) < lax.broadcasted_iota(jnp.int32, (tk, tq), 1)
        rows = [pl.ds(pl.multiple_of(j * tk, tk), tk) for j in js]
        nzs = [[_dot(k_ref[r, h * SB_HEAD_DIM:(h + 1) * SB_HEAD_DIM], qt_ref[h]) + bias_ref[hg * nh + h] for h in hs]
               for r in rows]
        lgs = [[_log2_one_minus_sigmoid(nz) for nz in row] for row in nzs]
        if masked:
            lgs = [[jnp.where(valid, lg, 0.0) for lg in row] for row in lgs]
        rincs = [[_dot(u2, jnp.concatenate(_split2(lg), axis=0)) for lg in row] for row in lgs]
        accs = list(accs)
        for b, j in enumerate(js):
            probs = [jnp.exp2((rincs[b][h] + accs[h]) - nzs[b][h]) for h in hs]
            if masked:
                probs = [jnp.where(valid, a, 0.0) for a in probs]
            for h in hs:
                o_ref[h] += _dot(vt_ref[h, j], probs[h].astype(BF16))
            accs = [accs[h] + rincs[b][h][0:1, :] for h in hs]
        return tuple(accs)

    o_ref[...] = jnp.zeros_like(o_ref)
    accs = blocks([i], tuple(jnp.zeros((1, tq), F32) for _ in range(nh)), True)
    accs = lax.fori_loop(0, i // 2, lambda n, c: blocks([i - 1 - 2 * n, i - 2 - 2 * n], c, False), accs)
    lax.fori_loop(0, i % 2, lambda n, c: blocks([0], c, False), accs)


def _sb_prompt(q, k, v, bias):
    s = q.shape[0]
    hh, dh = SB_HEADS, SB_HEAD_DIM
    hg = SB_HEAD_GROUP
    nkb = s // SB_TK
    qt = (q * (-(dh ** -0.5) * LOG2E)).astype(BF16).reshape(s, hh, dh).transpose(1, 2, 0)
    kb = k.astype(BF16)
    vt = v.astype(BF16).reshape(nkb, SB_TK, hh, dh).transpose(2, 0, 3, 1)
    r = jnp.arange(SB_TK)
    u = (r[None, :] >= r[:, None]).astype(BF16)
    u2 = jnp.concatenate([u, u], axis=1)
    once = pl.Buffered(1)
    out_t = pl.pallas_call(
        _sbp_kernel,
        grid=(hh // hg, s // SB_TQ),
        in_specs=[pl.BlockSpec(memory_space=pltpu.SMEM),
                  pl.BlockSpec((hg, dh, SB_TQ), lambda g, i: (g, 0, i)),
                  pl.BlockSpec((s, hg * dh), lambda g, i: (0, g), pipeline_mode=once),
                  pl.BlockSpec((hg, nkb, dh, SB_TK), lambda g, i: (g, 0, 0, 0), pipeline_mode=once),
                  pl.BlockSpec((SB_TK, 2 * SB_TK), lambda g, i: (0, 0), pipeline_mode=once)],
        out_specs=pl.BlockSpec((hg, dh, SB_TQ), lambda g, i: (g, 0, i)),
        out_shape=jax.ShapeDtypeStruct((hh, dh, s), F32),
        compiler_params=_params("parallel", "arbitrary"),
        name="sb_prompt",
    )(bias.astype(F32) * (-LOG2E), qt, kb, vt, u2)
    return out_t.transpose(2, 0, 1).reshape(s, hh * dh)


def _sbs_kernel(pt_ref, qbd_ref, bias_ref, knew_ref, vnew_ref, *rest, pages):
    kc_refs, vc_refs = rest[:pages], rest[pages:2 * pages]
    u2_ref, o_ref, acc_scr, out_scr = rest[2 * pages:]
    p = pl.program_id(1)
    qbd = qbd_ref[...]
    bias = bias_ref[...]
    u2 = u2_ref[...]
    nrow, npos = bias.shape
    t = nrow // SB_HEADS

    def group(k_refs, v_refs, masked):
        kcat = jnp.concatenate([r[...].astype(BF16) for r in k_refs], axis=0)
        vcat = jnp.concatenate([r[...].astype(BF16) for r in v_refs], axis=0)
        zall = _dot_nt(qbd, kcat)
        acc = acc_scr[...]
        probs = []
        for gi in range(len(k_refs)):
            z = zall[:, gi * npos:(gi + 1) * npos] + bias
            lg = _neg_softplus(z)
            if masked:
                tok = lax.broadcasted_iota(jnp.int32, (nrow, npos), 0) % t
                valid = lax.broadcasted_iota(jnp.int32, (nrow, npos), 1) < tok
                lg = jnp.where(valid, lg, 0.0)
            hi, lo = _split2(lg)
            rinc = _dot(jnp.concatenate([hi, lo], axis=1), u2)
            a = jnp.exp(z + rinc + acc)
            if masked:
                a = jnp.where(valid, a, 0.0)
            probs.append(a.astype(BF16))
            acc = acc + jnp.sum(lg, axis=-1, keepdims=True)
        out_scr[...] += _dot(jnp.concatenate(probs, axis=1), vcat)
        acc_scr[...] = acc

    @pl.when(p == 0)
    def _():
        acc_scr[...] = jnp.zeros_like(acc_scr)
        out_scr[...] = jnp.zeros_like(out_scr)
        group([knew_ref], [vnew_ref], True)

    group(kc_refs, vc_refs, False)

    @pl.when(p == pl.num_programs(1) - 1)
    def _():
        for h in range(SB_HEADS):
            sl = slice(h * SB_HEAD_DIM, (h + 1) * SB_HEAD_DIM)
            o_ref[:, sl] = out_scr[h * t:(h + 1) * t, sl]


def _sb_sample(q, k, v, bias, cache_k, cache_v, page_ids):
    nb, t, w = q.shape
    hh, dh = SB_HEADS, SB_HEAD_DIM
    npg = page_ids.shape[1]
    psz = cache_k.shape[1]
    pgs = min(SB_PAGE_GROUP, npg)
    qs = (q * (dh ** -0.5)).astype(BF16).reshape(nb, t, hh, dh)
    eye = jnp.eye(hh, dtype=BF16)
    qbd = (qs.transpose(0, 2, 1, 3)[:, :, :, None, :] * eye[None, :, None, :, None]).reshape(nb, hh * t, w)
    bias_rep = jnp.broadcast_to(jnp.repeat(bias.astype(F32), t)[:, None], (hh * t, psz))
    pad = ((0, 0), (0, psz - t), (0, 0))
    knew, vnew = jnp.pad(k, pad), jnp.pad(v, pad)
    r = jnp.arange(psz)
    u = (r[:, None] >= r[None, :]).astype(BF16)
    u2 = jnp.concatenate([u, u], axis=0)
    per_seq = lambda b, p, pt: (b, 0, 0)
    cache = [pl.BlockSpec((None, psz, w), lambda b, p, pt, gi=gi: (pt[b, npg - 1 - (p * pgs + gi)], 0, 0))
             for gi in range(pgs)]
    return pl.pallas_call(
        functools.partial(_sbs_kernel, pages=pgs),
        grid_spec=pltpu.PrefetchScalarGridSpec(
            num_scalar_prefetch=1,
            grid=(nb, npg // pgs),
            in_specs=[pl.BlockSpec((None, hh * t, w), per_seq),
                      pl.BlockSpec((hh * t, psz), lambda b, p, pt: (0, 0)),
                      pl.BlockSpec((None, psz, w), per_seq),
                      pl.BlockSpec((None, psz, w), per_seq)]
                     + cache + cache
                     + [pl.BlockSpec((2 * psz, psz), lambda b, p, pt: (0, 0))],
            out_specs=pl.BlockSpec((None, t, w), per_seq),
            scratch_shapes=[pltpu.VMEM((hh * t, psz), F32), pltpu.VMEM((hh * t, w), F32)],
        ),
        out_shape=jax.ShapeDtypeStruct((nb, t, w), F32),
        compiler_params=_params("parallel", "arbitrary"),
        name="sb_sample",
    )(page_ids, qbd, bias_rep, knew, vnew, *([cache_k] * pgs), *([cache_v] * pgs), u2)


def _merge_kernel(oa_ref, ob_ref, zg_ref, ga_ref, gb_ref, x_ref, nw_ref, woa_ref, wob_ref, wout_ref,
                  g1_ref, b1_ref, h_ref, hb_ref, *, alpha):
    ob = ob_ref[...]
    zg = zg_ref[...]
    nw = nw_ref[...]
    parts = []
    for h in range(GDN_HEADS):
        sl = slice(h * GDN_DIM, (h + 1) * GDN_DIM)
        oh = ob[:, sl]
        zh = zg[:, sl]
        rn = oh * lax.rsqrt(jnp.mean(oh * oh, axis=-1, keepdims=True) + RMS_EPS)
        parts.append((rn * nw * (zh * jax.nn.sigmoid(zh))).astype(BF16))
    obn = jnp.concatenate(parts, axis=-1)
    merged = (jax.nn.sigmoid(ga_ref[...]) * _dot(oa_ref[...], woa_ref[...])
              + jax.nn.sigmoid(gb_ref[...]) * _dot(obn, wob_ref[...]))
    y = alpha * x_ref[...] + _dot(merged.astype(BF16), wout_ref[...])
    hn = _layernorm(y, g1_ref[...], b1_ref[...])
    h_ref[...] = hn
    hb_ref[...] = hn.astype(BF16)


def _merge(oa, ob, zg, ga, gb, x, nw, woa, wob, wout, g1, b1, alpha, bm, name):
    m, d = x.shape
    w = oa.shape[1]
    bm = min(bm, m)
    row = lambda i: (i, 0)
    fix = lambda i: (0, 0)
    once = functools.partial(pl.BlockSpec, index_map=fix, pipeline_mode=pl.Buffered(1))
    return pl.pallas_call(
        functools.partial(_merge_kernel, alpha=alpha),
        grid=(m // bm,),
        in_specs=[pl.BlockSpec((bm, w), row), pl.BlockSpec((bm, w), row), pl.BlockSpec((bm, w), row),
                  pl.BlockSpec((bm, d), row), pl.BlockSpec((bm, d), row), pl.BlockSpec((bm, d), row),
                  pl.BlockSpec((1, GDN_DIM), fix),
                  once((w, d)), once((w, d)), once((d, d)),
                  pl.BlockSpec((1, d), fix), pl.BlockSpec((1, d), fix)],
        out_specs=[pl.BlockSpec((bm, d), row), pl.BlockSpec((bm, d), row)],
        out_shape=[jax.ShapeDtypeStruct((m, d), F32), jax.ShapeDtypeStruct((m, d), BF16)],
        compiler_params=_params("parallel"),
        name=name,
    )(oa, ob, zg, ga, gb, x, nw, woa, wob, wout, g1, b1)


def _ffn_kernel(hb_ref, h_ref, wg_ref, wu_ref, wd_ref, g2_ref, b2_ref, y_ref, acc_ref, *, alpha):
    f = pl.program_id(1)
    hb = hb_ref[...]
    gate = _dot(hb, wg_ref[...])
    up = _dot(hb, wu_ref[...])
    act = (gate * jax.nn.sigmoid(gate) * up).astype(BF16)
    part = _dot(act, wd_ref[...])

    @pl.when(f == 0)
    def _():
        acc_ref[...] = part

    @pl.when(f != 0)
    def _():
        acc_ref[...] += part

    @pl.when(f == pl.num_programs(1) - 1)
    def _():
        y_ref[...] = _layernorm(alpha * h_ref[...] + acc_ref[...], g2_ref[...], b2_ref[...])


def _ffn(hb, h, wg, wu, wd, g2, b2, alpha, bm, bf, name):
    m, d = h.shape
    dff = wg.shape[1]
    bm = min(bm, m)
    row = lambda i, f: (i, 0)
    fix = lambda i, f: (0, 0)
    return pl.pallas_call(
        functools.partial(_ffn_kernel, alpha=alpha),
        grid=(m // bm, dff // bf),
        in_specs=[pl.BlockSpec((bm, d), row), pl.BlockSpec((bm, d), row),
                  pl.BlockSpec((d, bf), lambda i, f: (0, f)), pl.BlockSpec((d, bf), lambda i, f: (0, f)),
                  pl.BlockSpec((bf, d), lambda i, f: (f, 0)),
                  pl.BlockSpec((1, d), fix), pl.BlockSpec((1, d), fix)],
        out_specs=pl.BlockSpec((bm, d), row),
        out_shape=jax.ShapeDtypeStruct((m, d), F32),
        scratch_shapes=[pltpu.VMEM((bm, d), F32)],
        compiler_params=_params("parallel", "arbitrary"),
        name=name,
    )(hb, h, wg, wu, wd, g2, b2)


def _layer(x, sb_mix, conv_prev, s_prev, wts, alpha, tag, bm):
    nb, t, d = x.shape
    m = nb * t
    w = SB_HEADS * SB_HEAD_DIM
    xf = x.reshape(m, d)
    xb = xf.astype(BF16)
    mm = lambda wt, nm: _matmul(xb, wt, bm, 1024, f"proj_{nm}_{tag}")
    qa, ka, va = mm(wts["w_q"], "q"), mm(wts["w_k"], "k"), mm(wts["w_v"], "v")
    conv_in = mm(wts["w_conv"], "conv").reshape(nb, t, 3 * w)
    zg, ga, gb, ba = mm(wts["w_z"], "z"), mm(wts["w_ga"], "ga"), mm(wts["w_gb"], "gb"), mm(wts["w_ba"], "ba")

    o_a = sb_mix(qa, ka, va)

    prev8 = jnp.pad(conv_prev.astype(F32), ((0, 0), (8 - (CONV_WIDTH - 1), 0), (0, 0)))
    qn, kn, vv, beta, g = _gdn_pre(conv_in, prev8, wts["conv_w"], ba.reshape(nb, t, LANES), wts["gparams"],
                                   256, f"gdn_pre_{tag}")
    tp = -(-t // GDN_CHUNK) * GDN_CHUNK
    if tp != t:
        padt = lambda a: jnp.pad(a, ((0, 0), (0, tp - t), (0, 0)))
        qn, kn, vv, beta, g = map(padt, (qn, kn, vv, beta, g))
    o_b, s_new = _gdn(qn, kn, vv, beta, g, s_prev.astype(F32), tag)
    o_b = o_b[:, :t].reshape(m, w)

    h, hb = _merge(o_a.astype(BF16), o_b, zg, ga, gb, xf, wts["norm_w"], wts["w_o_a"], wts["w_o_b"], wts["w_out"],
                   wts["ln1_g"], wts["ln1_b"], alpha, 256, f"merge_{tag}")
    y = _ffn(hb, h, wts["w_g"], wts["w_u"], wts["w_down"], wts["ln2_g"], wts["ln2_b"], alpha, 512, 512,
             f"ffn_{tag}")
    conv_new = jnp.concatenate([conv_prev.astype(F32), conv_in], axis=1)[:, -(CONV_WIDTH - 1):]
    return (y.reshape(nb, t, d),
            (ka.reshape(nb, t, SB_HEADS, SB_HEAD_DIM), va.reshape(nb, t, SB_HEADS, SB_HEAD_DIM), s_new, conv_new))


def _prep_weights(w_in, sb_bias, conv_w, a_log, dt_bias, gdn_norm_w, w_o_a, w_o_b, w_out,
                  ln1_g, ln1_b, w_gu, w_down, ln2_g, ln2_b):
    w = SB_HEADS * SB_HEAD_DIM
    d = w_in.shape[0]
    dff = w_down.shape[0]
    hh = GDN_HEADS
    o = 0
    cols = {}
    for nm, width in (("w_q", w), ("w_k", w), ("w_v", w), ("w_conv", 3 * w), ("w_z", w),
                      ("w_ba", 2 * hh), ("w_ga", d), ("w_gb", d)):
        cols[nm] = w_in[:, o:o + width].astype(BF16)
        o += width
    cols["w_ba"] = jnp.pad(cols["w_ba"], ((0, 0), (0, LANES - 2 * hh)))
    gp = jnp.zeros((8, LANES), F32)
    gp = gp.at[0, hh:2 * hh].set(-jnp.exp(a_log.astype(F32))).at[1, hh:2 * hh].set(dt_bias.astype(F32))
    cols.update(
        sb_bias=sb_bias, conv_w=conv_w.astype(F32), gparams=gp,
        norm_w=gdn_norm_w.astype(F32).reshape(1, GDN_DIM),
        w_o_a=w_o_a.astype(BF16), w_o_b=w_o_b.astype(BF16), w_out=w_out.astype(BF16),
        ln1_g=ln1_g.reshape(1, d), ln1_b=ln1_b.reshape(1, d),
        w_g=w_gu[:, :dff].astype(BF16), w_u=w_gu[:, dff:].astype(BF16), w_down=w_down.astype(BF16),
        ln2_g=ln2_g.reshape(1, d), ln2_b=ln2_b.reshape(1, d))
    return cols


def kernel(x_prompt, x_sample, cache_k, cache_v, state_gdn, state_conv, page_table, w_in, sb_bias, conv_w,
           a_log, dt_bias, gdn_norm_w, w_o_a, w_o_b, w_out, ln1_g, ln1_b, w_gu, w_down, ln2_g, ln2_b):
    depth = w_in.shape[0]
    alpha = (2.0 * depth) ** 0.25
    w = SB_HEADS * SB_HEAD_DIM
    nphys, psz = cache_k.shape[1], cache_k.shape[2]
    ck = cache_k.reshape(depth * nphys, psz, w)
    cv = cache_v.reshape(depth * nphys, psz, w)
    yp, ys = x_prompt, x_sample
    outs = [[] for _ in range(8)]
    for l in range(depth):
        wts = _prep_weights(w_in[l], sb_bias[l], conv_w[l], a_log[l], dt_bias[l], gdn_norm_w[l], w_o_a[l],
                            w_o_b[l], w_out[l], ln1_g[l], ln1_b[l], w_gu[l], w_down[l], ln2_g[l], ln2_b[l])
        nbp, tp = yp.shape[0], yp.shape[1]
        conv0 = jnp.zeros((nbp, CONV_WIDTH - 1, 3 * w), F32)
        s0 = jnp.zeros((nbp, GDN_HEADS, GDN_DIM, GDN_DIM), F32)

        def sb_p(q, k, v):
            return jnp.concatenate([_sb_prompt(q[b * tp:(b + 1) * tp], k[b * tp:(b + 1) * tp],
                                               v[b * tp:(b + 1) * tp], wts["sb_bias"]) for b in range(nbp)], axis=0)

        yp, (kp, vp, sp, cp) = _layer(yp, sb_p, conv0, s0, wts, alpha, "p", 1024)

        nbs, ts = ys.shape[0], ys.shape[1]
        page_ids = page_table + l * nphys

        def sb_s(q, k, v):
            r3 = lambda a: a.reshape(nbs, ts, w)
            return _sb_sample(r3(q), r3(k), r3(v), wts["sb_bias"], ck, cv, page_ids).reshape(nbs * ts, w)

        ys, (ksm, vsm, ssm, csm) = _layer(ys, sb_s, state_conv[l], state_gdn[l], wts, alpha, "s", 256)
        for lst, val in zip(outs, (kp, vp, sp, cp, ksm, vsm, ssm, csm)):
            lst.append(val)
    return (yp, ys) + tuple(jnp.stack(o) for o in outs)
```

```python
import functools

import jax
import jax.numpy as jnp
from jax import lax
from jax.experimental import pallas as pl
from jax.experimental.pallas import tpu as pltpu

F32 = jnp.float32
BF16 = jnp.bfloat16

LANES = 128
VMEM_LIMIT = 56 * 1024 * 1024

SB_HEADS = 8
SB_HEAD_DIM = 128
GDN_HEADS = 8
GDN_DIM = 128
CONV_WIDTH = 4
GDN_CHUNK = 64
LN_EPS = 1e-5
RMS_EPS = 1e-6
L2_EPS = 1e-6

SB_TQ = 256
SB_TK = 256
SB_HEAD_GROUP = 4
SB_PAGE_GROUP = 8
GDN_CHUNK_GROUP = 8


def _params(*sem):
    return pltpu.CompilerParams(dimension_semantics=sem, vmem_limit_bytes=VMEM_LIMIT)


def _dot(a, b):
    return jnp.dot(a, b, preferred_element_type=F32)


def _dot_nt(a, b):
    return lax.dot_general(a, b, (((1,), (1,)), ((), ())), preferred_element_type=F32)


def _dot_tn(a, b):
    return lax.dot_general(a, b, (((0,), (0,)), ((), ())), preferred_element_type=F32)


def _split2(x):
    hi = x.astype(BF16)
    lo = (x - hi.astype(F32)).astype(BF16)
    return hi, lo


def _split3(x):
    hi = x.astype(BF16)
    r = x - hi.astype(F32)
    mid = r.astype(BF16)
    lo = (r - mid.astype(F32)).astype(BF16)
    return hi, mid, lo


def _dot_sel(m01, x):
    x1, x2, x3 = _split3(x)
    return _dot(m01, x1) + (_dot(m01, x2) + _dot(m01, x3))


def _neg_softplus(z):
    return jnp.minimum(-z, 0.0) - jnp.log(1.0 + jnp.exp(-jnp.abs(z)))


LOG2E = 1.4426950408889634


def _log2_one_minus_sigmoid(nz):
    nabs = pltpu.bitcast(pltpu.bitcast(nz, jnp.uint32) | jnp.uint32(0x80000000), F32)
    return jnp.minimum(nz, 0.0) - jnp.log(1.0 + jnp.exp2(nabs)) * LOG2E


def _softplus(z):
    return jnp.maximum(z, 0.0) + jnp.log1p(jnp.exp(-jnp.abs(z)))


def _layernorm(x, g, b):
    mu = jnp.mean(x, axis=-1, keepdims=True)
    xc = x - mu
    var = jnp.mean(xc * xc, axis=-1, keepdims=True)
    return xc * lax.rsqrt(var + LN_EPS) * g + b


def _mm_kernel(x_ref, w_ref, o_ref):
    o_ref[...] = _dot(x_ref[...], w_ref[...])


def _matmul(x, w, bm, bn, name):
    m, k = x.shape
    n = w.shape[1]
    bm, bn = min(bm, m), min(bn, n)
    return pl.pallas_call(
        _mm_kernel,
        grid=(n // bn, m // bm),
        in_specs=[pl.BlockSpec((bm, k), lambda j, i: (i, 0)),
                  pl.BlockSpec((k, bn), lambda j, i: (0, j))],
        out_specs=pl.BlockSpec((bm, bn), lambda j, i: (i, j)),
        out_shape=jax.ShapeDtypeStruct((m, n), F32),
        compiler_params=_params("parallel", "parallel"),
        name=name,
    )(x, w)


def _gdn_pre_kernel(cur_ref, halo_ref, prev_ref, cw_ref, ba_ref, gp_ref, q_ref, k_ref, v_ref, beta_ref, g_ref):
    j = pl.program_id(1)
    cur = cur_ref[...]
    halo = jnp.where(j == 0, prev_ref[...], halo_ref[...])
    ext = jnp.concatenate([halo, cur], axis=0)
    cw = cw_ref[...]
    conv = cur * cw[CONV_WIDTH - 1:CONV_WIDTH]
    for d in range(1, CONV_WIDTH):
        conv = conv + pltpu.roll(ext, d, 0)[8:] * cw[CONV_WIDTH - 1 - d:CONV_WIDTH - d]
    c = conv * jax.nn.sigmoid(conv)
    w = GDN_HEADS * GDN_DIM
    for h in range(GDN_HEADS):
        sl = slice(h * GDN_DIM, (h + 1) * GDN_DIM)
        qh = c[:, sl]
        q_ref[:, sl] = qh * lax.rsqrt(jnp.sum(qh * qh, axis=-1, keepdims=True) + L2_EPS) * (GDN_DIM ** -0.5)
        kh = c[:, w + h * GDN_DIM:w + (h + 1) * GDN_DIM]
        k_ref[:, sl] = kh * lax.rsqrt(jnp.sum(kh * kh, axis=-1, keepdims=True) + L2_EPS)
    v_ref[...] = c[:, 2 * w:]
    ba = ba_ref[...]
    gp = gp_ref[...]
    beta = jax.nn.sigmoid(ba)
    g = gp[0:1] * _softplus(ba + gp[1:2])
    for h in range(GDN_HEADS):
        sl = slice(h * GDN_DIM, (h + 1) * GDN_DIM)
        beta_ref[:, sl] = jnp.broadcast_to(beta[:, h:h + 1], (beta.shape[0], GDN_DIM))
        g_ref[:, sl] = jnp.broadcast_to(g[:, GDN_HEADS + h:GDN_HEADS + h + 1], (g.shape[0], GDN_DIM))


def _gdn_pre(conv_in, conv_prev8, conv_w, ba, gparams, bm, name):
    nb, t, c3 = conv_in.shape
    w = c3 // 3
    bm = min(bm, t)
    hb = bm // 8
    row = lambda b, j: (b, j, 0)
    out = jax.ShapeDtypeStruct((nb, t, w), F32)
    return pl.pallas_call(
        _gdn_pre_kernel,
        grid=(nb, t // bm),
        in_specs=[pl.BlockSpec((None, bm, c3), row),
                  pl.BlockSpec((None, 8, c3), lambda b, j: (b, jnp.maximum(j * hb - 1, 0), 0)),
                  pl.BlockSpec((None, 8, c3), lambda b, j: (b, 0, 0)),
                  pl.BlockSpec((CONV_WIDTH, c3), lambda b, j: (0, 0)),
                  pl.BlockSpec((None, bm, LANES), row),
                  pl.BlockSpec((8, LANES), lambda b, j: (0, 0))],
        out_specs=[pl.BlockSpec((None, bm, w), row)] * 5,
        out_shape=[out] * 5,
        compiler_params=_params("parallel", "parallel"),
        name=name,
    )(conv_in, conv_in, conv_prev8, conv_w, ba, gparams)


def _gdn_prep_kernel(q_ref, k_ref, v_ref, beta_ref, g_ref, u_ref, w_ref, qd_ref, kd_ref, intra_ref, gl_ref,
                     *, chunks):
    n = GDN_CHUNK
    ri = lax.broadcasted_iota(jnp.int32, (n, n), 0)
    ci = lax.broadcasted_iota(jnp.int32, (n, n), 1)
    incl = ri >= ci
    strict = ri > ci
    tri = incl.astype(BF16)
    upper = (ri <= ci).astype(F32)
    ones = jnp.ones((n, n), BF16)
    lane_pad = jnp.zeros((n, GDN_DIM - n), F32)

    def chunk(c, carry):
        rows = pl.ds(pl.multiple_of(c * n, n), n)
        hs = range(GDN_HEADS)
        sls = [slice(h * GDN_DIM, (h + 1) * GDN_DIM) for h in hs]
        ks = [k_ref[rows, sl] for sl in sls]
        betas = [beta_ref[rows, sl] for sl in sls]
        gs = [g_ref[rows, sl] for sl in sls]
        gcums = [_dot_sel(tri, g) for g in gs]
        grows = [_dot_sel(ones, g[:, :n] * upper) for g in gs]
        kbfs = [k.astype(BF16) for k in ks]
        kbs = [k * beta for k, beta in zip(ks, betas)]
        kks = [_dot_nt(kb.astype(BF16), kbf) for kb, kbf in zip(kbs, kbfs)]
        qks = [_dot_nt(q_ref[rows, sl].astype(BF16), kbf) for sl, kbf in zip(sls, kbfs)]
        gams = [jnp.where(incl, jnp.exp(jnp.where(incl, gc[:, :n] - gr, 0.0)), 0.0) for gc, gr in zip(gcums, grows)]
        egs = [jnp.exp(gc) for gc in gcums]
        mps = [-jnp.where(strict, kk * gam, 0.0) for kk, gam in zip(kks, gams)]
        sols = [jnp.concatenate([v_ref[rows, sl] * beta, kb * eg], axis=-1)
                for sl, beta, kb, eg in zip(sls, betas, kbs, egs)]
        span = 1
        while span < n:
            mpbs = [mp.astype(BF16) for mp in mps]
            upds = [_dot(mpb, jnp.concatenate(_split2(sol), axis=-1)) for mpb, sol in zip(mpbs, sols)]
            sols = [sol + (upd[:, :2 * GDN_DIM] + upd[:, 2 * GDN_DIM:]) for sol, upd in zip(sols, upds)]
            span *= 2
            if span < n:
                mps = [_dot(mpb, mpb) for mpb in mpbs]
        for h in hs:
            sl = sls[h]
            glast = gcums[h][n - 1:n, :]
            u_ref[rows, sl] = sols[h][:, :GDN_DIM]
            w_ref[rows, sl] = sols[h][:, GDN_DIM:].astype(BF16)
            intra = jnp.where(incl, qks[h] * gams[h], 0.0)
            intra_ref[rows, sl] = jnp.concatenate([intra, lane_pad], axis=-1).astype(BF16)
            qd_ref[rows, sl] = (q_ref[rows, sl] * egs[h]).astype(BF16)
            kd_ref[rows, sl] = (ks[h] * jnp.exp(glast - gcums[h])).astype(BF16)
            gl_ref[c, h:h + 1, :] = jnp.exp(glast)
        return carry

    lax.fori_loop(0, chunks, chunk, 0)


def _gdn_scan_kernel(u_ref, w_ref, qd_ref, kd_ref, intra_ref, gl_ref, s0_ref, o_ref, sfin_ref, s_scr, *, chunks):
    n = GDN_CHUNK
    cg = pl.program_id(1)

    @pl.when(cg == 0)
    def _():
        s_scr[...] = s0_ref[...]

    def chunk(c, carry):
        rows = pl.ds(pl.multiple_of(c * n, n), n)
        hs = range(GDN_HEADS)
        sls = [slice(h * GDN_DIM, (h + 1) * GDN_DIM) for h in hs]
        ss = [s_scr[h] for h in hs]
        rs = [_dot(jnp.concatenate([w_ref[rows, sl], qd_ref[rows, sl]], axis=0), s.astype(BF16))
              for sl, s in zip(sls, ss)]
        vbs = [(u_ref[rows, sl] - r[:n]).astype(BF16) for sl, r in zip(sls, rs)]
        upds = [_dot_tn(kd_ref[rows, sl], vb) for sl, vb in zip(sls, vbs)]
        for h in hs:
            s_scr[h] = ss[h] * gl_ref[c, h:h + 1, :] + upds[h]
        for h in hs:
            o_ref[rows, sls[h]] = rs[h][n:] + _dot(intra_ref[rows, h * GDN_DIM:h * GDN_DIM + n], vbs[h])
        return carry

    lax.fori_loop(0, chunks, chunk, 0)

    @pl.when(cg == pl.num_programs(1) - 1)
    def _():
        sfin_ref[...] = s_scr[...]


def _gdn(q, k, v, beta, g, s0, tag):
    nb, t, w = q.shape
    nc = t // GDN_CHUNK
    cgs = min(GDN_CHUNK_GROUP, nc)
    rows = cgs * GDN_CHUNK
    tok = pl.BlockSpec((None, rows, w), lambda b, c: (b, c, 0))
    gl = pl.BlockSpec((None, cgs, GDN_HEADS, GDN_DIM), lambda b, c: (b, c, 0, 0))
    st = pl.BlockSpec((None, GDN_HEADS, GDN_DIM, GDN_DIM), lambda b, c: (b, 0, 0, 0))
    tok_f32 = jax.ShapeDtypeStruct((nb, t, w), F32)
    tok_bf16 = jax.ShapeDtypeStruct((nb, t, w), BF16)
    gl_shape = jax.ShapeDtypeStruct((nb, nc, GDN_HEADS, GDN_DIM), F32)
    u, wy, qd, kd, intra, glast = pl.pallas_call(
        functools.partial(_gdn_prep_kernel, chunks=cgs),
        grid=(nb, nc // cgs),
        in_specs=[tok] * 5,
        out_specs=[tok] * 5 + [gl],
        out_shape=[tok_f32] + [tok_bf16] * 4 + [gl_shape],
        compiler_params=_params("parallel", "parallel"),
        name=f"gdn_prep_{tag}",
    )(q, k, v, beta, g)
    return pl.pallas_call(
        functools.partial(_gdn_scan_kernel, chunks=cgs),
        grid=(nb, nc // cgs),
        in_specs=[tok] * 5 + [gl, st],
        out_specs=[tok, st],
        out_shape=[tok_f32, jax.ShapeDtypeStruct((nb, GDN_HEADS, GDN_DIM, GDN_DIM), F32)],
        scratch_shapes=[pltpu.VMEM((GDN_HEADS, GDN_DIM, GDN_DIM), F32)],
        compiler_params=_params("parallel", "arbitrary"),
        name=f"gdn_scan_{tag}",
    )(u, wy, qd, kd, intra, glast, s0)


def _sbp_kernel(bias_ref, qt_ref, k_ref, vt_ref, u2_ref, o_ref):
    hg = pl.program_id(0)
    i = pl.program_id(1)
    nh = qt_ref.shape[0]
    u2 = u2_ref[...]
    tk, tq = SB_TK, SB_TQ

    def blocks(js, accs, masked):
        hs = range(nh)
        if masked:
            valid = lax.broadcasted_iota(jnp.int32, (tk, tq), 0) < lax.broadcasted_iota(jnp.int32, (tk, tq), 1)
        rows = [pl.ds(pl.multiple_of(j * tk, tk), tk) for j in js]
        nzs = [[_dot(k_ref[r, h * SB_HEAD_DIM:(h + 1) * SB_HEAD_DIM], qt_ref[h]) + bias_ref[hg * nh + h] for h in hs]
               for r in rows]
        lgs = [[_log2_one_minus_sigmoid(nz) for nz in row] for row in nzs]
        if masked:
            lgs = [[jnp.where(valid, lg, 0.0) for lg in row] for row in lgs]
        rincs = [[_dot(u2, jnp.concatenate(_split2(lg), axis=0)) for lg in row] for row in lgs]
        accs = list(accs)
        for b, j in enumerate(js):
            probs = [jnp.exp2((rincs[b][h] + accs[h]) - nzs[b][h]) for h in hs]
            if masked:
                probs = [jnp.where(valid, a, 0.0) for a in probs]
            for h in hs:
                o_ref[h] += _dot(vt_ref[h, j], probs[h].astype(BF16))
            accs = [accs[h] + rincs[b][h][0:1, :] for h in hs]
        return tuple(accs)

    o_ref[...] = jnp.zeros_like(o_ref)
    accs = blocks([i], tuple(jnp.zeros((1, tq), F32) for _ in range(nh)), True)
    accs = lax.fori_loop(0, i // 2, lambda n, c: blocks([i - 1 - 2 * n, i - 2 - 2 * n], c, False), accs)
    lax.fori_loop(0, i % 2, lambda n, c: blocks([0], c, False), accs)


def _sb_prompt(q, k, v, bias):
    s = q.shape[0]
    hh, dh = SB_HEADS, SB_HEAD_DIM
    hg = SB_HEAD_GROUP
    nkb = s // SB_TK
    qt = (q * (-(dh ** -0.5) * LOG2E)).astype(BF16).reshape(s, hh, dh).transpose(1, 2, 0)
    kb = k.astype(BF16)
    vt = v.astype(BF16).reshape(nkb, SB_TK, hh, dh).transpose(2, 0, 3, 1)
    r = jnp.arange(SB_TK)
    u = (r[None, :] >= r[:, None]).astype(BF16)
    u2 = jnp.concatenate([u, u], axis=1)
    once = pl.Buffered(1)
    out_t = pl.pallas_call(
        _sbp_kernel,
        grid=(hh // hg, s // SB_TQ),
        in_specs=[pl.BlockSpec(memory_space=pltpu.SMEM),
                  pl.BlockSpec((hg, dh, SB_TQ), lambda g, i: (g, 0, i)),
                  pl.BlockSpec((s, hg * dh), lambda g, i: (0, g), pipeline_mode=once),
                  pl.BlockSpec((hg, nkb, dh, SB_TK), lambda g, i: (g, 0, 0, 0), pipeline_mode=once),
                  pl.BlockSpec((SB_TK, 2 * SB_TK), lambda g, i: (0, 0), pipeline_mode=once)],
        out_specs=pl.BlockSpec((hg, dh, SB_TQ), lambda g, i: (g, 0, i)),
        out_shape=jax.ShapeDtypeStruct((hh, dh, s), F32),
        compiler_params=_params("parallel", "arbitrary"),
        name="sb_prompt",
    )(bias.astype(F32) * (-LOG2E), qt, kb, vt, u2)
    return out_t.transpose(2, 0, 1).reshape(s, hh * dh)


def _sbs_kernel(pt_ref, qbd_ref, bias_ref, knew_ref, vnew_ref, *rest, pages):
    kc_refs, vc_refs = rest[:pages], rest[pages:2 * pages]
    u2_ref, o_ref, acc_scr, out_scr = rest[2 * pages:]
    p = pl.program_id(1)
    qbd = qbd_ref[...]
    bias = bias_ref[...]
    u2 = u2_ref[...]
    nrow, npos = bias.shape
    t = nrow // SB_HEADS

    def page_rows(ref):
        if ref.shape[1] != SB_HEAD_DIM:
            return ref[...].astype(BF16)
        return jnp.concatenate([ref[pl.ds(h, npos, stride=SB_HEADS), :].astype(BF16) for h in range(SB_HEADS)],
                               axis=1)

    def group(k_refs, v_refs, masked):
        kcat = jnp.concatenate([page_rows(r) for r in k_refs], axis=0)
        vcat = jnp.concatenate([page_rows(r) for r in v_refs], axis=0)
        zall = _dot_nt(qbd, kcat)
        acc = acc_scr[...]
        probs = []
        for gi in range(len(k_refs)):
            z = zall[:, gi * npos:(gi + 1) * npos] + bias
            lg = _neg_softplus(z)
            if masked:
                tok = lax.broadcasted_iota(jnp.int32, (nrow, npos), 0) % t
                valid = lax.broadcasted_iota(jnp.int32, (nrow, npos), 1) < tok
                lg = jnp.where(valid, lg, 0.0)
            hi, lo = _split2(lg)
            rinc = _dot(jnp.concatenate([hi, lo], axis=1), u2)
            a = jnp.exp(z + rinc + acc)
            if masked:
                a = jnp.where(valid, a, 0.0)
            probs.append(a.astype(BF16))
            acc = acc + jnp.sum(lg, axis=-1, keepdims=True)
        out_scr[...] += _dot(jnp.concatenate(probs, axis=1), vcat)
        acc_scr[...] = acc

    @pl.when(p == 0)
    def _():
        acc_scr[...] = jnp.zeros_like(acc_scr)
        out_scr[...] = jnp.zeros_like(out_scr)
        group([knew_ref], [vnew_ref], True)

    group(kc_refs, vc_refs, False)

    @pl.when(p == pl.num_programs(1) - 1)
    def _():
        for h in range(SB_HEADS):
            sl = slice(h * SB_HEAD_DIM, (h + 1) * SB_HEAD_DIM)
            o_ref[:, sl] = out_scr[h * t:(h + 1) * t, sl]


def _sb_sample(q, k, v, bias, cache_k, cache_v, page_ids):
    nb, t, w = q.shape
    hh, dh = SB_HEADS, SB_HEAD_DIM
    npg = page_ids.shape[1]
    psz = cache_k.shape[1] // hh
    pgs = min(SB_PAGE_GROUP, npg)
    qs = (q * (dh ** -0.5)).astype(BF16).reshape(nb, t, hh, dh)
    eye = jnp.eye(hh, dtype=BF16)
    qbd = (qs.transpose(0, 2, 1, 3)[:, :, :, None, :] * eye[None, :, None, :, None]).reshape(nb, hh * t, w)
    bias_rep = jnp.broadcast_to(jnp.repeat(bias.astype(F32), t)[:, None], (hh * t, psz))
    pad = ((0, 0), (0, psz - t), (0, 0))
    knew, vnew = jnp.pad(k, pad), jnp.pad(v, pad)
    r = jnp.arange(psz)
    u = (r[:, None] >= r[None, :]).astype(BF16)
    u2 = jnp.concatenate([u, u], axis=0)
    per_seq = lambda b, p, pt: (b, 0, 0)
    cache = [pl.BlockSpec((None, psz * hh, dh), lambda b, p, pt, gi=gi: (pt[b, npg - 1 - (p * pgs + gi)], 0, 0))
             for gi in range(pgs)]
    return pl.pallas_call(
        functools.partial(_sbs_kernel, pages=pgs),
        grid_spec=pltpu.PrefetchScalarGridSpec(
            num_scalar_prefetch=1,
            grid=(nb, npg // pgs),
            in_specs=[pl.BlockSpec((None, hh * t, w), per_seq),
                      pl.BlockSpec((hh * t, psz), lambda b, p, pt: (0, 0)),
                      pl.BlockSpec((None, psz, w), per_seq),
                      pl.BlockSpec((None, psz, w), per_seq)]
                     + cache + cache
                     + [pl.BlockSpec((2 * psz, psz), lambda b, p, pt: (0, 0))],
            out_specs=pl.BlockSpec((None, t, w), per_seq),
            scratch_shapes=[pltpu.VMEM((hh * t, psz), F32), pltpu.VMEM((hh * t, w), F32)],
        ),
        out_shape=jax.ShapeDtypeStruct((nb, t, w), F32),
        compiler_params=_params("parallel", "arbitrary"),
        name="sb_sample",
    )(page_ids, qbd, bias_rep, knew, vnew, *([cache_k] * pgs), *([cache_v] * pgs), u2)


def _merge_kernel(oa_ref, ob_ref, zg_ref, ga_ref, gb_ref, x_ref, nw_ref, woa_ref, wob_ref, wout_ref,
                  g1_ref, b1_ref, h_ref, hb_ref, *, alpha):
    ob = ob_ref[...]
    zg = zg_ref[...]
    nw = nw_ref[...]
    parts = []
    for h in range(GDN_HEADS):
        sl = slice(h * GDN_DIM, (h + 1) * GDN_DIM)
        oh = ob[:, sl]
        zh = zg[:, sl]
        rn = oh * lax.rsqrt(jnp.mean(oh * oh, axis=-1, keepdims=True) + RMS_EPS)
        parts.append((rn * nw * (zh * jax.nn.sigmoid(zh))).astype(BF16))
    obn = jnp.concatenate(parts, axis=-1)
    merged = (jax.nn.sigmoid(ga_ref[...]) * _dot(oa_ref[...], woa_ref[...])
              + jax.nn.sigmoid(gb_ref[...]) * _dot(obn, wob_ref[...]))
    y = alpha * x_ref[...] + _dot(merged.astype(BF16), wout_ref[...])
    hn = _layernorm(y, g1_ref[...], b1_ref[...])
    h_ref[...] = hn
    hb_ref[...] = hn.astype(BF16)


def _merge(oa, ob, zg, ga, gb, x, nw, woa, wob, wout, g1, b1, alpha, bm, name):
    m, d = x.shape
    w = oa.shape[1]
    bm = min(bm, m)
    row = lambda i: (i, 0)
    fix = lambda i: (0, 0)
    once = functools.partial(pl.BlockSpec, index_map=fix, pipeline_mode=pl.Buffered(1))
    return pl.pallas_call(
        functools.partial(_merge_kernel, alpha=alpha),
        grid=(m // bm,),
        in_specs=[pl.BlockSpec((bm, w), row), pl.BlockSpec((bm, w), row), pl.BlockSpec((bm, w), row),
                  pl.BlockSpec((bm, d), row), pl.BlockSpec((bm, d), row), pl.BlockSpec((bm, d), row),
                  pl.BlockSpec((1, GDN_DIM), fix),
                  once((w, d)), once((w, d)), once((d, d)),
                  pl.BlockSpec((1, d), fix), pl.BlockSpec((1, d), fix)],
        out_specs=[pl.BlockSpec((bm, d), row), pl.BlockSpec((bm, d), row)],
        out_shape=[jax.ShapeDtypeStruct((m, d), F32), jax.ShapeDtypeStruct((m, d), BF16)],
        compiler_params=_params("parallel"),
        name=name,
    )(oa, ob, zg, ga, gb, x, nw, woa, wob, wout, g1, b1)


def _ffn_kernel(hb_ref, h_ref, wg_ref, wu_ref, wd_ref, g2_ref, b2_ref, y_ref, acc_ref, *, alpha):
    f = pl.program_id(1)
    hb = hb_ref[...]
    gate = _dot(hb, wg_ref[...])
    up = _dot(hb, wu_ref[...])
    act = (gate * jax.nn.sigmoid(gate) * up).astype(BF16)
    part = _dot(act, wd_ref[...])

    @pl.when(f == 0)
    def _():
        acc_ref[...] = part

    @pl.when(f != 0)
    def _():
        acc_ref[...] += part

    @pl.when(f == pl.num_programs(1) - 1)
    def _():
        y_ref[...] = _layernorm(alpha * h_ref[...] + acc_ref[...], g2_ref[...], b2_ref[...])


def _ffn(hb, h, wg, wu, wd, g2, b2, alpha, bm, bf, name):
    m, d = h.shape
    dff = wg.shape[1]
    bm = min(bm, m)
    row = lambda i, f: (i, 0)
    fix = lambda i, f: (0, 0)
    return pl.pallas_call(
        functools.partial(_ffn_kernel, alpha=alpha),
        grid=(m // bm, dff // bf),
        in_specs=[pl.BlockSpec((bm, d), row), pl.BlockSpec((bm, d), row),
                  pl.BlockSpec((d, bf), lambda i, f: (0, f)), pl.BlockSpec((d, bf), lambda i, f: (0, f)),
                  pl.BlockSpec((bf, d), lambda i, f: (f, 0)),
                  pl.BlockSpec((1, d), fix), pl.BlockSpec((1, d), fix)],
        out_specs=pl.BlockSpec((bm, d), row),
        out_shape=jax.ShapeDtypeStruct((m, d), F32),
        scratch_shapes=[pltpu.VMEM((bm, d), F32)],
        compiler_params=_params("parallel", "arbitrary"),
        name=name,
    )(hb, h, wg, wu, wd, g2, b2)


def _layer(x, sb_mix, conv_prev, s_prev, wts, alpha, tag, bm):
    nb, t, d = x.shape
    m = nb * t
    w = SB_HEADS * SB_HEAD_DIM
    xf = x.reshape(m, d)
    xb = xf.astype(BF16)
    mm = lambda wt, nm: _matmul(xb, wt, bm, 1024, f"proj_{nm}_{tag}")
    qa, ka, va = mm(wts["w_q"], "q"), mm(wts["w_k"], "k"), mm(wts["w_v"], "v")
    conv_in = mm(wts["w_conv"], "conv").reshape(nb, t, 3 * w)
    zg, ga, gb, ba = mm(wts["w_z"], "z"), mm(wts["w_ga"], "ga"), mm(wts["w_gb"], "gb"), mm(wts["w_ba"], "ba")

    o_a = sb_mix(qa, ka, va)

    prev8 = jnp.pad(conv_prev.astype(F32), ((0, 0), (8 - (CONV_WIDTH - 1), 0), (0, 0)))
    qn, kn, vv, beta, g = _gdn_pre(conv_in, prev8, wts["conv_w"], ba.reshape(nb, t, LANES), wts["gparams"],
                                   256, f"gdn_pre_{tag}")
    tp = -(-t // GDN_CHUNK) * GDN_CHUNK
    if tp != t:
        padt = lambda a: jnp.pad(a, ((0, 0), (0, tp - t), (0, 0)))
        qn, kn, vv, beta, g = map(padt, (qn, kn, vv, beta, g))
    o_b, s_new = _gdn(qn, kn, vv, beta, g, s_prev.astype(F32), tag)
    o_b = o_b[:, :t].reshape(m, w)

    h, hb = _merge(o_a.astype(BF16), o_b, zg, ga, gb, xf, wts["norm_w"], wts["w_o_a"], wts["w_o_b"], wts["w_out"],
                   wts["ln1_g"], wts["ln1_b"], alpha, 256, f"merge_{tag}")
    y = _ffn(hb, h, wts["w_g"], wts["w_u"], wts["w_down"], wts["ln2_g"], wts["ln2_b"], alpha, 512, 512,
             f"ffn_{tag}")
    conv_new = jnp.concatenate([conv_prev.astype(F32), conv_in], axis=1)[:, -(CONV_WIDTH - 1):]
    return (y.reshape(nb, t, d),
            (ka.reshape(nb, t, SB_HEADS, SB_HEAD_DIM), va.reshape(nb, t, SB_HEADS, SB_HEAD_DIM), s_new, conv_new))


def _prep_weights(w_in, sb_bias, conv_w, a_log, dt_bias, gdn_norm_w, w_o_a, w_o_b, w_out,
                  ln1_g, ln1_b, w_gu, w_down, ln2_g, ln2_b):
    w = SB_HEADS * SB_HEAD_DIM
    d = w_in.shape[0]
    dff = w_down.shape[0]
    hh = GDN_HEADS
    o = 0
    cols = {}
    for nm, width in (("w_q", w), ("w_k", w), ("w_v", w), ("w_conv", 3 * w), ("w_z", w),
                      ("w_ba", 2 * hh), ("w_ga", d), ("w_gb", d)):
        cols[nm] = w_in[:, o:o + width].astype(BF16)
        o += width
    cols["w_ba"] = jnp.pad(cols["w_ba"], ((0, 0), (0, LANES - 2 * hh)))
    gp = jnp.zeros((8, LANES), F32)
    gp = gp.at[0, hh:2 * hh].set(-jnp.exp(a_log.astype(F32))).at[1, hh:2 * hh].set(dt_bias.astype(F32))
    cols.update(
        sb_bias=sb_bias, conv_w=conv_w.astype(F32), gparams=gp,
        norm_w=gdn_norm_w.astype(F32).reshape(1, GDN_DIM),
        w_o_a=w_o_a.astype(BF16), w_o_b=w_o_b.astype(BF16), w_out=w_out.astype(BF16),
        ln1_g=ln1_g.reshape(1, d), ln1_b=ln1_b.reshape(1, d),
        w_g=w_gu[:, :dff].astype(BF16), w_u=w_gu[:, dff:].astype(BF16), w_down=w_down.astype(BF16),
        ln2_g=ln2_g.reshape(1, d), ln2_b=ln2_b.reshape(1, d))
    return cols


def kernel(x_prompt, x_sample, cache_k, cache_v, state_gdn, state_conv, page_table, w_in, sb_bias, conv_w,
           a_log, dt_bias, gdn_norm_w, w_o_a, w_o_b, w_out, ln1_g, ln1_b, w_gu, w_down, ln2_g, ln2_b):
    depth = w_in.shape[0]
    alpha = (2.0 * depth) ** 0.25
    w = SB_HEADS * SB_HEAD_DIM
    nphys, psz = cache_k.shape[1], cache_k.shape[2]
    ck = cache_k.reshape(depth * nphys, psz * SB_HEADS, SB_HEAD_DIM)
    cv = cache_v.reshape(depth * nphys, psz * SB_HEADS, SB_HEAD_DIM)
    yp, ys = x_prompt, x_sample
    outs = [[] for _ in range(8)]
    for l in range(depth):
        wts = _prep_weights(w_in[l], sb_bias[l], conv_w[l], a_log[l], dt_bias[l], gdn_norm_w[l], w_o_a[l],
                            w_o_b[l], w_out[l], ln1_g[l], ln1_b[l], w_gu[l], w_down[l], ln2_g[l], ln2_b[l])
        nbp, tp = yp.shape[0], yp.shape[1]
        conv0 = jnp.zeros((nbp, CONV_WIDTH - 1, 3 * w), F32)
        s0 = jnp.zeros((nbp, GDN_HEADS, GDN_DIM, GDN_DIM), F32)

        def sb_p(q, k, v):
            return jnp.concatenate([_sb_prompt(q[b * tp:(b + 1) * tp], k[b * tp:(b + 1) * tp],
                                               v[b * tp:(b + 1) * tp], wts["sb_bias"]) for b in range(nbp)], axis=0)

        yp, (kp, vp, sp, cp) = _layer(yp, sb_p, conv0, s0, wts, alpha, "p", 1024)

        nbs, ts = ys.shape[0], ys.shape[1]
        page_ids = page_table + l * nphys

        def sb_s(q, k, v):
            r3 = lambda a: a.reshape(nbs, ts, w)
            return _sb_sample(r3(q), r3(k), r3(v), wts["sb_bias"], ck, cv, page_ids).reshape(nbs * ts, w)

        ys, (ksm, vsm, ssm, csm) = _layer(ys, sb_s, state_conv[l], state_gdn[l], wts, alpha, "s", 256)
        for lst, val in zip(outs, (kp, vp, sp, cp, ksm, vsm, ssm, csm)):
            lst.append(val)
    return (yp, ys) + tuple(jnp.stack(o) for o in outs)
```

```python
import functools

import jax
import jax.numpy as jnp
from jax import lax
from jax.experimental import pallas as pl
from jax.experimental.pallas import tpu as pltpu

F32 = jnp.float32
BF16 = jnp.bfloat16

LANES = 128
VMEM_LIMIT = 56 * 1024 * 1024

SB_HEADS = 8
SB_HEAD_DIM = 128
GDN_HEADS = 8
GDN_DIM = 128
CONV_WIDTH = 4
GDN_CHUNK = 64
LN_EPS = 1e-5
RMS_EPS = 1e-6
L2_EPS = 1e-6

SB_TQ = 256
SB_TK = 128
SB_HEAD_GROUP = 4
SB_KEY_BLOCKS = 16
SB_SKEW = 2
SB_PAGE_GROUP = 16
GDN_SPLIT_STEPS = 3
GDN_CHUNK_GROUP = 8


def _params(*sem):
    return pltpu.CompilerParams(dimension_semantics=sem, vmem_limit_bytes=VMEM_LIMIT)


def _dot(a, b):
    return jnp.dot(a, b, preferred_element_type=F32)


def _dot_nt(a, b):
    return lax.dot_general(a, b, (((1,), (1,)), ((), ())), preferred_element_type=F32)


def _dot_tn(a, b):
    return lax.dot_general(a, b, (((0,), (0,)), ((), ())), preferred_element_type=F32)


def _split2(x):
    hi = x.astype(BF16)
    lo = (x - hi.astype(F32)).astype(BF16)
    return hi, lo


def _split3(x):
    hi = x.astype(BF16)
    r = x - hi.astype(F32)
    mid = r.astype(BF16)
    lo = (r - mid.astype(F32)).astype(BF16)
    return hi, mid, lo


def _dot_sel(m01, x):
    x1, x2, x3 = _split3(x)
    return _dot(m01, x1) + (_dot(m01, x2) + _dot(m01, x3))


def _neg_softplus(z):
    return jnp.minimum(-z, 0.0) - jnp.log(1.0 + jnp.exp(-jnp.abs(z)))


LOG2E = 1.4426950408889634


def _log2_one_minus_sigmoid(nz):
    nabs = pltpu.bitcast(pltpu.bitcast(nz, jnp.uint32) | jnp.uint32(0x80000000), F32)
    return jnp.minimum(nz, 0.0) - jnp.log(1.0 + jnp.exp2(nabs)) * LOG2E


def _softplus(z):
    return jnp.maximum(z, 0.0) + jnp.log1p(jnp.exp(-jnp.abs(z)))


def _layernorm(x, g, b):
    mu = jnp.mean(x, axis=-1, keepdims=True)
    xc = x - mu
    var = jnp.mean(xc * xc, axis=-1, keepdims=True)
    return xc * lax.rsqrt(var + LN_EPS) * g + b


def _mm_kernel(x_ref, w_ref, o_ref):
    o_ref[...] = _dot(x_ref[...], w_ref[...])


def _matmul(x, w, bm, bn, name):
    m, k = x.shape
    n = w.shape[1]
    bm, bn = min(bm, m), min(bn, n)
    return pl.pallas_call(
        _mm_kernel,
        grid=(n // bn, m // bm),
        in_specs=[pl.BlockSpec((bm, k), lambda j, i: (i, 0)),
                  pl.BlockSpec((k, bn), lambda j, i: (0, j))],
        out_specs=pl.BlockSpec((bm, bn), lambda j, i: (i, j)),
        out_shape=jax.ShapeDtypeStruct((m, n), F32),
        compiler_params=_params("parallel", "parallel"),
        name=name,
    )(x, w)


def _gdn_pre_kernel(cur_ref, halo_ref, prev_ref, cw_ref, ba_ref, gp_ref, q_ref, k_ref, v_ref, beta_ref, g_ref):
    j = pl.program_id(1)
    cur = cur_ref[...]
    halo = jnp.where(j == 0, prev_ref[...], halo_ref[...])
    ext = jnp.concatenate([halo, cur], axis=0)
    cw = cw_ref[...]
    conv = cur * cw[CONV_WIDTH - 1:CONV_WIDTH]
    for d in range(1, CONV_WIDTH):
        conv = conv + pltpu.roll(ext, d, 0)[8:] * cw[CONV_WIDTH - 1 - d:CONV_WIDTH - d]
    c = conv * jax.nn.sigmoid(conv)
    w = GDN_HEADS * GDN_DIM
    for h in range(GDN_HEADS):
        sl = slice(h * GDN_DIM, (h + 1) * GDN_DIM)
        qh = c[:, sl]
        q_ref[:, sl] = qh * lax.rsqrt(jnp.sum(qh * qh, axis=-1, keepdims=True) + L2_EPS) * (GDN_DIM ** -0.5)
        kh = c[:, w + h * GDN_DIM:w + (h + 1) * GDN_DIM]
        k_ref[:, sl] = kh * lax.rsqrt(jnp.sum(kh * kh, axis=-1, keepdims=True) + L2_EPS)
    v_ref[...] = c[:, 2 * w:]
    ba = ba_ref[...]
    gp = gp_ref[...]
    beta = jax.nn.sigmoid(ba)
    g = gp[0:1] * _softplus(ba + gp[1:2])
    for h in range(GDN_HEADS):
        sl = slice(h * GDN_DIM, (h + 1) * GDN_DIM)
        beta_ref[:, sl] = jnp.broadcast_to(beta[:, h:h + 1], (beta.shape[0], GDN_DIM))
        g_ref[:, sl] = jnp.broadcast_to(g[:, GDN_HEADS + h:GDN_HEADS + h + 1], (g.shape[0], GDN_DIM))


def _gdn_pre(conv_in, conv_prev8, conv_w, ba, gparams, bm, name):
    nb, t, c3 = conv_in.shape
    w = c3 // 3
    bm = min(bm, t)
    hb = bm // 8
    row = lambda b, j: (b, j, 0)
    out = jax.ShapeDtypeStruct((nb, t, w), F32)
    return pl.pallas_call(
        _gdn_pre_kernel,
        grid=(nb, t // bm),
        in_specs=[pl.BlockSpec((None, bm, c3), row),
                  pl.BlockSpec((None, 8, c3), lambda b, j: (b, jnp.maximum(j * hb - 1, 0), 0)),
                  pl.BlockSpec((None, 8, c3), lambda b, j: (b, 0, 0)),
                  pl.BlockSpec((CONV_WIDTH, c3), lambda b, j: (0, 0)),
                  pl.BlockSpec((None, bm, LANES), row),
                  pl.BlockSpec((8, LANES), lambda b, j: (0, 0))],
        out_specs=[pl.BlockSpec((None, bm, w), row)] * 5,
        out_shape=[out] * 5,
        compiler_params=_params("parallel", "parallel"),
        name=name,
    )(conv_in, conv_in, conv_prev8, conv_w, ba, gparams)


def _gdn_prep_kernel(q_ref, k_ref, v_ref, beta_ref, g_ref, u_ref, w_ref, qd_ref, kd_ref, intra_ref, gl_ref,
                     *, chunks):
    n = GDN_CHUNK
    ri = lax.broadcasted_iota(jnp.int32, (n, n), 0)
    ci = lax.broadcasted_iota(jnp.int32, (n, n), 1)
    incl = ri >= ci
    strict = ri > ci
    tri = incl.astype(BF16)
    upper = (ri <= ci).astype(F32)
    ones = jnp.ones((n, n), BF16)
    lane_pad = jnp.zeros((n, GDN_DIM - n), F32)

    def chunk(c, carry):
        rows = pl.ds(pl.multiple_of(c * n, n), n)
        hs = range(GDN_HEADS)
        sls = [slice(h * GDN_DIM, (h + 1) * GDN_DIM) for h in hs]
        ks = [k_ref[rows, sl] for sl in sls]
        betas = [beta_ref[rows, sl] for sl in sls]
        gs = [g_ref[rows, sl] for sl in sls]
        gcums = [_dot_sel(tri, g) for g in gs]
        grows = [_dot_sel(ones, g[:, :n] * upper) for g in gs]
        kbfs = [k.astype(BF16) for k in ks]
        kbs = [k * beta for k, beta in zip(ks, betas)]
        kks = [_dot_nt(kb.astype(BF16), kbf) for kb, kbf in zip(kbs, kbfs)]
        qks = [_dot_nt(q_ref[rows, sl].astype(BF16), kbf) for sl, kbf in zip(sls, kbfs)]
        gams = [jnp.where(incl, jnp.exp(jnp.where(incl, gc[:, :n] - gr, 0.0)), 0.0) for gc, gr in zip(gcums, grows)]
        egs = [jnp.exp(gc) for gc in gcums]
        mps = [-jnp.where(strict, kk * gam, 0.0) for kk, gam in zip(kks, gams)]
        sols = [jnp.concatenate([v_ref[rows, sl] * beta, kb * eg], axis=-1)
                for sl, beta, kb, eg in zip(sls, betas, kbs, egs)]
        span = 1
        while span < n:
            mpbs = [mp.astype(BF16) for mp in mps]
            if span < (1 << GDN_SPLIT_STEPS):
                upds = [_dot(mpb, jnp.concatenate(_split2(sol), axis=-1)) for mpb, sol in zip(mpbs, sols)]
                sols = [sol + (upd[:, :2 * GDN_DIM] + upd[:, 2 * GDN_DIM:]) for sol, upd in zip(sols, upds)]
            else:
                sols = [sol + _dot(mpb, sol.astype(BF16)) for mpb, sol in zip(mpbs, sols)]
            span *= 2
            if span < n:
                mps = [_dot(mpb, mpb) for mpb in mpbs]
        for h in hs:
            sl = sls[h]
            glast = gcums[h][n - 1:n, :]
            u_ref[rows, sl] = sols[h][:, :GDN_DIM]
            w_ref[rows, sl] = sols[h][:, GDN_DIM:].astype(BF16)
            intra = jnp.where(incl, qks[h] * gams[h], 0.0)
            intra_ref[rows, sl] = jnp.concatenate([intra, lane_pad], axis=-1).astype(BF16)
            qd_ref[rows, sl] = (q_ref[rows, sl] * egs[h]).astype(BF16)
            kd_ref[rows, sl] = (ks[h] * jnp.exp(glast - gcums[h])).astype(BF16)
            gl_ref[c, h:h + 1, :] = jnp.exp(glast)
        return carry

    lax.fori_loop(0, chunks, chunk, 0)


def _gdn_scan_kernel(u_ref, w_ref, qd_ref, kd_ref, intra_ref, gl_ref, s0_ref, o_ref, sfin_ref, s_scr, *, chunks):
    n = GDN_CHUNK
    cg = pl.program_id(1)

    @pl.when(cg == 0)
    def _():
        s_scr[...] = s0_ref[...]

    def chunk(c, carry):
        rows = pl.ds(pl.multiple_of(c * n, n), n)
        hs = range(GDN_HEADS)
        sls = [slice(h * GDN_DIM, (h + 1) * GDN_DIM) for h in hs]
        ss = [s_scr[h] for h in hs]
        rs = [_dot(jnp.concatenate([w_ref[rows, sl], qd_ref[rows, sl]], axis=0), s.astype(BF16))
              for sl, s in zip(sls, ss)]
        vbs = [(u_ref[rows, sl] - r[:n]).astype(BF16) for sl, r in zip(sls, rs)]
        upds = [_dot_tn(kd_ref[rows, sl], vb) for sl, vb in zip(sls, vbs)]
        for h in hs:
            s_scr[h] = ss[h] * gl_ref[c, h:h + 1, :] + upds[h]
        for h in hs:
            o_ref[rows, sls[h]] = rs[h][n:] + _dot(intra_ref[rows, h * GDN_DIM:h * GDN_DIM + n], vbs[h])
        return carry

    lax.fori_loop(0, chunks, chunk, 0)

    @pl.when(cg == pl.num_programs(1) - 1)
    def _():
        sfin_ref[...] = s_scr[...]


def _gdn(q, k, v, beta, g, s0, tag):
    nb, t, w = q.shape
    nc = t // GDN_CHUNK
    cgs = min(GDN_CHUNK_GROUP, nc)
    rows = cgs * GDN_CHUNK
    tok = pl.BlockSpec((None, rows, w), lambda b, c: (b, c, 0))
    gl = pl.BlockSpec((None, cgs, GDN_HEADS, GDN_DIM), lambda b, c: (b, c, 0, 0))
    st = pl.BlockSpec((None, GDN_HEADS, GDN_DIM, GDN_DIM), lambda b, c: (b, 0, 0, 0))
    tok_f32 = jax.ShapeDtypeStruct((nb, t, w), F32)
    tok_bf16 = jax.ShapeDtypeStruct((nb, t, w), BF16)
    gl_shape = jax.ShapeDtypeStruct((nb, nc, GDN_HEADS, GDN_DIM), F32)
    u, wy, qd, kd, intra, glast = pl.pallas_call(
        functools.partial(_gdn_prep_kernel, chunks=cgs),
        grid=(nb, nc // cgs),
        in_specs=[tok] * 5,
        out_specs=[tok] * 5 + [gl],
        out_shape=[tok_f32] + [tok_bf16] * 4 + [gl_shape],
        compiler_params=_params("parallel", "parallel"),
        name=f"gdn_prep_{tag}",
    )(q, k, v, beta, g)
    return pl.pallas_call(
        functools.partial(_gdn_scan_kernel, chunks=cgs),
        grid=(nb, nc // cgs),
        in_specs=[tok] * 5 + [gl, st],
        out_specs=[tok, st],
        out_shape=[tok_f32, jax.ShapeDtypeStruct((nb, GDN_HEADS, GDN_DIM, GDN_DIM), F32)],
        scratch_shapes=[pltpu.VMEM((GDN_HEADS, GDN_DIM, GDN_DIM), F32)],
        compiler_params=_params("parallel", "arbitrary"),
        name=f"gdn_scan_{tag}",
    )(u, wy, qd, kd, intra, glast, s0)


def _sbp_kernel(bias_ref, qt_ref, k_ref, vt_ref, u2_ref, o_ref):
    hg = pl.program_id(0)
    i = pl.program_id(1)
    nh = qt_ref.shape[0]
    u2 = u2_ref[...]
    tk, tq = SB_TK, SB_TQ

    def blocks(js, accs, offs):
        valid = [None if o is None else
                 (lax.broadcasted_iota(jnp.int32, (tk, tq), 0) + o) < lax.broadcasted_iota(jnp.int32, (tk, tq), 1)
                 for o in offs]
        chains = [(b, h) for b in range(len(js)) for h in range(nh)]
        rows = [pl.ds(pl.multiple_of(j * tk, tk), tk) for j in js]
        accs = list(accs)
        nz, lg, rinc = {}, {}, {}

        def st_qk(c):
            b, h = c
            nz[c] = _dot(k_ref[rows[b], h * SB_HEAD_DIM:(h + 1) * SB_HEAD_DIM], qt_ref[h]) + bias_ref[hg * nh + h]

        def st_lg(c):
            v = _log2_one_minus_sigmoid(nz[c])
            lg[c] = v if valid[c[0]] is None else jnp.where(valid[c[0]], v, 0.0)

        def st_sum(c):
            rinc[c] = _dot(u2, jnp.concatenate(_split2(lg.pop(c)), axis=0))

        def st_av(c):
            b, h = c
            a = jnp.exp2((rinc[c] + accs[h]) - nz.pop(c))
            if valid[b] is not None:
                a = jnp.where(valid[b], a, 0.0)
            o_ref[h] += _dot(vt_ref[h, js[b]], a.astype(BF16))
            accs[h] = accs[h] + rinc.pop(c)[0:1, :]

        stages = [st_qk, st_lg, st_sum, st_av]
        for step in range(len(chains) + (len(stages) - 1) * SB_SKEW):
            for si, st in enumerate(stages):
                ci = step - si * SB_SKEW
                if 0 <= ci < len(chains):
                    st(chains[ci])
        return tuple(accs)

    o_ref[...] = jnp.zeros_like(o_ref)
    per = tq // tk
    nb = SB_KEY_BLOCKS
    first = i * per
    accs = blocks([first + per - 1 - d for d in range(per)], tuple(jnp.zeros((1, tq), F32) for _ in range(nh)),
                  [(per - 1 - d) * tk for d in range(per)])
    accs = lax.fori_loop(0, first // nb,
                         lambda n, c: blocks([first - 1 - nb * n - d for d in range(nb)], c, [None] * nb), accs)
    done = (first // nb) * nb
    size = nb // 2
    while size >= per:
        take = ((first - done) // size) % 2
        start = first - done - 1
        accs = lax.fori_loop(0, take, lambda n, c, start=start, size=size:
                             blocks([start - d for d in range(size)], c, [None] * size), accs)
        done = done + take * size
        size //= 2


def _sb_prompt(q, k, v, bias):
    s = q.shape[0]
    hh, dh = SB_HEADS, SB_HEAD_DIM
    hg = SB_HEAD_GROUP
    nkb = s // SB_TK
    qt = (q * (-(dh ** -0.5) * LOG2E)).astype(BF16).reshape(s, hh, dh).transpose(1, 2, 0)
    kb = k.astype(BF16)
    vt = v.astype(BF16).reshape(nkb, SB_TK, hh, dh).transpose(2, 0, 3, 1)
    r = jnp.arange(SB_TK)
    u = (r[None, :] >= r[:, None]).astype(BF16)
    u2 = jnp.concatenate([u, u], axis=1)
    once = pl.Buffered(1)
    out_t = pl.pallas_call(
        _sbp_kernel,
        grid=(hh // hg, s // SB_TQ),
        in_specs=[pl.BlockSpec(memory_space=pltpu.SMEM),
                  pl.BlockSpec((hg, dh, SB_TQ), lambda g, i: (g, 0, i)),
                  pl.BlockSpec((s, hg * dh), lambda g, i: (0, g), pipeline_mode=once),
                  pl.BlockSpec((hg, nkb, dh, SB_TK), lambda g, i: (g, 0, 0, 0), pipeline_mode=once),
                  pl.BlockSpec((SB_TK, 2 * SB_TK), lambda g, i: (0, 0), pipeline_mode=once)],
        out_specs=pl.BlockSpec((hg, dh, SB_TQ), lambda g, i: (g, 0, i)),
        out_shape=jax.ShapeDtypeStruct((hh, dh, s), F32),
        compiler_params=_params("parallel", "arbitrary"),
        name="sb_prompt",
    )(bias.astype(F32) * (-LOG2E), qt, kb, vt, u2)
    return out_t.transpose(2, 0, 1).reshape(s, hh * dh)


def _sbs_kernel(pt_ref, qbd_ref, bias_ref, knew_ref, vnew_ref, *rest, pages):
    kc_refs, vc_refs = rest[:pages], rest[pages:2 * pages]
    u2_ref, o_ref, acc_scr, out_scr = rest[2 * pages:]
    p = pl.program_id(1)
    qbd = qbd_ref[...]
    bias = bias_ref[...]
    u2 = u2_ref[...]
    nrow, npos = bias.shape
    t = nrow // SB_HEADS

    def page_rows(ref):
        if ref.shape[1] != SB_HEAD_DIM:
            return ref[...].astype(BF16)
        return jnp.concatenate([ref[pl.ds(h, npos, stride=SB_HEADS), :].astype(BF16) for h in range(SB_HEADS)],
                               axis=1)

    def group(k_refs, v_refs, masked):
        kcat = jnp.concatenate([page_rows(r) for r in k_refs], axis=0)
        vcat = jnp.concatenate([page_rows(r) for r in v_refs], axis=0)
        zall = _dot_nt(qbd, kcat)
        acc = acc_scr[...]
        probs = []
        for gi in range(len(k_refs)):
            z = zall[:, gi * npos:(gi + 1) * npos] + bias
            lg = _neg_softplus(z)
            if masked:
                tok = lax.broadcasted_iota(jnp.int32, (nrow, npos), 0) % t
                valid = lax.broadcasted_iota(jnp.int32, (nrow, npos), 1) < tok
                lg = jnp.where(valid, lg, 0.0)
            hi, lo = _split2(lg)
            rinc = _dot(jnp.concatenate([hi, lo], axis=1), u2)
            a = jnp.exp(z + rinc + acc)
            if masked:
                a = jnp.where(valid, a, 0.0)
            probs.append(a.astype(BF16))
            acc = acc + jnp.sum(lg, axis=-1, keepdims=True)
        out_scr[...] += _dot(jnp.concatenate(probs, axis=1), vcat)
        acc_scr[...] = acc

    @pl.when(p == 0)
    def _():
        acc_scr[...] = jnp.zeros_like(acc_scr)
        out_scr[...] = jnp.zeros_like(out_scr)
        group([knew_ref], [vnew_ref], True)

    group(kc_refs, vc_refs, False)

    @pl.when(p == pl.num_programs(1) - 1)
    def _():
        for h in range(SB_HEADS):
            sl = slice(h * SB_HEAD_DIM, (h + 1) * SB_HEAD_DIM)
            o_ref[:, sl] = out_scr[h * t:(h + 1) * t, sl]


def _sb_sample(q, k, v, bias, cache_k, cache_v, page_ids):
    nb, t, w = q.shape
    hh, dh = SB_HEADS, SB_HEAD_DIM
    npg = page_ids.shape[1]
    psz = cache_k.shape[1] // hh
    pgs = min(SB_PAGE_GROUP, npg)
    qs = (q * (dh ** -0.5)).astype(BF16).reshape(nb, t, hh, dh)
    eye = jnp.eye(hh, dtype=BF16)
    qbd = (qs.transpose(0, 2, 1, 3)[:, :, :, None, :] * eye[None, :, None, :, None]).reshape(nb, hh * t, w)
    bias_rep = jnp.broadcast_to(jnp.repeat(bias.astype(F32), t)[:, None], (hh * t, psz))
    pad = ((0, 0), (0, psz - t), (0, 0))
    knew, vnew = jnp.pad(k, pad), jnp.pad(v, pad)
    r = jnp.arange(psz)
    u = (r[:, None] >= r[None, :]).astype(BF16)
    u2 = jnp.concatenate([u, u], axis=0)
    per_seq = lambda b, p, pt: (b, 0, 0)
    cache = [pl.BlockSpec((None, psz * hh, dh), lambda b, p, pt, gi=gi: (pt[b, npg - 1 - (p * pgs + gi)], 0, 0))
             for gi in range(pgs)]
    return pl.pallas_call(
        functools.partial(_sbs_kernel, pages=pgs),
        grid_spec=pltpu.PrefetchScalarGridSpec(
            num_scalar_prefetch=1,
            grid=(nb, npg // pgs),
            in_specs=[pl.BlockSpec((None, hh * t, w), per_seq),
                      pl.BlockSpec((hh * t, psz), lambda b, p, pt: (0, 0)),
                      pl.BlockSpec((None, psz, w), per_seq),
                      pl.BlockSpec((None, psz, w), per_seq)]
                     + cache + cache
                     + [pl.BlockSpec((2 * psz, psz), lambda b, p, pt: (0, 0))],
            out_specs=pl.BlockSpec((None, t, w), per_seq),
            scratch_shapes=[pltpu.VMEM((hh * t, psz), F32), pltpu.VMEM((hh * t, w), F32)],
        ),
        out_shape=jax.ShapeDtypeStruct((nb, t, w), F32),
        compiler_params=_params("parallel", "arbitrary"),
        name="sb_sample",
    )(page_ids, qbd, bias_rep, knew, vnew, *([cache_k] * pgs), *([cache_v] * pgs), u2)


def _merge_kernel(oa_ref, ob_ref, zg_ref, ga_ref, gb_ref, x_ref, nw_ref, woa_ref, wob_ref, wout_ref,
                  g1_ref, b1_ref, h_ref, hb_ref, *, alpha):
    ob = ob_ref[...]
    zg = zg_ref[...]
    nw = nw_ref[...]
    parts = []
    for h in range(GDN_HEADS):
        sl = slice(h * GDN_DIM, (h + 1) * GDN_DIM)
        oh = ob[:, sl]
        zh = zg[:, sl]
        rn = oh * lax.rsqrt(jnp.mean(oh * oh, axis=-1, keepdims=True) + RMS_EPS)
        parts.append((rn * nw * (zh * jax.nn.sigmoid(zh))).astype(BF16))
    obn = jnp.concatenate(parts, axis=-1)
    merged = (jax.nn.sigmoid(ga_ref[...]) * _dot(oa_ref[...], woa_ref[...])
              + jax.nn.sigmoid(gb_ref[...]) * _dot(obn, wob_ref[...]))
    y = alpha * x_ref[...] + _dot(merged.astype(BF16), wout_ref[...])
    hn = _layernorm(y, g1_ref[...], b1_ref[...])
    h_ref[...] = hn
    hb_ref[...] = hn.astype(BF16)


def _merge(oa, ob, zg, ga, gb, x, nw, woa, wob, wout, g1, b1, alpha, bm, name):
    m, d = x.shape
    w = oa.shape[1]
    bm = min(bm, m)
    row = lambda i: (i, 0)
    fix = lambda i: (0, 0)
    once = functools.partial(pl.BlockSpec, index_map=fix, pipeline_mode=pl.Buffered(1))
    return pl.pallas_call(
        functools.partial(_merge_kernel, alpha=alpha),
        grid=(m // bm,),
        in_specs=[pl.BlockSpec((bm, w), row), pl.BlockSpec((bm, w), row), pl.BlockSpec((bm, w), row),
                  pl.BlockSpec((bm, d), row), pl.BlockSpec((bm, d), row), pl.BlockSpec((bm, d), row),
                  pl.BlockSpec((1, GDN_DIM), fix),
                  once((w, d)), once((w, d)), once((d, d)),
                  pl.BlockSpec((1, d), fix), pl.BlockSpec((1, d), fix)],
        out_specs=[pl.BlockSpec((bm, d), row), pl.BlockSpec((bm, d), row)],
        out_shape=[jax.ShapeDtypeStruct((m, d), F32), jax.ShapeDtypeStruct((m, d), BF16)],
        compiler_params=_params("parallel"),
        name=name,
    )(oa, ob, zg, ga, gb, x, nw, woa, wob, wout, g1, b1)


def _ffn_kernel(hb_ref, h_ref, wg_ref, wu_ref, wd_ref, g2_ref, b2_ref, y_ref, acc_ref, *, alpha):
    f = pl.program_id(1)
    hb = hb_ref[...]
    gate = _dot(hb, wg_ref[...])
    up = _dot(hb, wu_ref[...])
    act = (gate * jax.nn.sigmoid(gate) * up).astype(BF16)
    part = _dot(act, wd_ref[...])

    @pl.when(f == 0)
    def _():
        acc_ref[...] = part

    @pl.when(f != 0)
    def _():
        acc_ref[...] += part

    @pl.when(f == pl.num_programs(1) - 1)
    def _():
        y_ref[...] = _layernorm(alpha * h_ref[...] + acc_ref[...], g2_ref[...], b2_ref[...])


def _ffn(hb, h, wg, wu, wd, g2, b2, alpha, bm, bf, name):
    m, d = h.shape
    dff = wg.shape[1]
    bm = min(bm, m)
    row = lambda i, f: (i, 0)
    fix = lambda i, f: (0, 0)
    return pl.pallas_call(
        functools.partial(_ffn_kernel, alpha=alpha),
        grid=(m // bm, dff // bf),
        in_specs=[pl.BlockSpec((bm, d), row), pl.BlockSpec((bm, d), row),
                  pl.BlockSpec((d, bf), lambda i, f: (0, f)), pl.BlockSpec((d, bf), lambda i, f: (0, f)),
                  pl.BlockSpec((bf, d), lambda i, f: (f, 0)),
                  pl.BlockSpec((1, d), fix), pl.BlockSpec((1, d), fix)],
        out_specs=pl.BlockSpec((bm, d), row),
        out_shape=jax.ShapeDtypeStruct((m, d), F32),
        scratch_shapes=[pltpu.VMEM((bm, d), F32)],
        compiler_params=_params("parallel", "arbitrary"),
        name=name,
    )(hb, h, wg, wu, wd, g2, b2)


def _layer(x, sb_mix, conv_prev, s_prev, wts, alpha, tag, bm):
    nb, t, d = x.shape
    m = nb * t
    w = SB_HEADS * SB_HEAD_DIM
    xf = x.reshape(m, d)
    xb = xf.astype(BF16)
    mm = lambda wt, nm: _matmul(xb, wt, bm, 1024, f"proj_{nm}_{tag}")
    qa, ka, va = mm(wts["w_q"], "q"), mm(wts["w_k"], "k"), mm(wts["w_v"], "v")
    conv_in = mm(wts["w_conv"], "conv").reshape(nb, t, 3 * w)
    zg, ga, gb, ba = mm(wts["w_z"], "z"), mm(wts["w_ga"], "ga"), mm(wts["w_gb"], "gb"), mm(wts["w_ba"], "ba")

    o_a = sb_mix(qa, ka, va)

    prev8 = jnp.pad(conv_prev.astype(F32), ((0, 0), (8 - (CONV_WIDTH - 1), 0), (0, 0)))
    qn, kn, vv, beta, g = _gdn_pre(conv_in, prev8, wts["conv_w"], ba.reshape(nb, t, LANES), wts["gparams"],
                                   256, f"gdn_pre_{tag}")
    tp = -(-t // GDN_CHUNK) * GDN_CHUNK
    if tp != t:
        padt = lambda a: jnp.pad(a, ((0, 0), (0, tp - t), (0, 0)))
        qn, kn, vv, beta, g = map(padt, (qn, kn, vv, beta, g))
    o_b, s_new = _gdn(qn, kn, vv, beta, g, s_prev.astype(F32), tag)
    o_b = o_b[:, :t].reshape(m, w)

    h, hb = _merge(o_a.astype(BF16), o_b, zg, ga, gb, xf, wts["norm_w"], wts["w_o_a"], wts["w_o_b"], wts["w_out"],
                   wts["ln1_g"], wts["ln1_b"], alpha, 256, f"merge_{tag}")
    y = _ffn(hb, h, wts["w_g"], wts["w_u"], wts["w_down"], wts["ln2_g"], wts["ln2_b"], alpha, 512, 512,
             f"ffn_{tag}")
    conv_new = jnp.concatenate([conv_prev.astype(F32), conv_in], axis=1)[:, -(CONV_WIDTH - 1):]
    return (y.reshape(nb, t, d),
            (ka.reshape(nb, t, SB_HEADS, SB_HEAD_DIM), va.reshape(nb, t, SB_HEADS, SB_HEAD_DIM), s_new, conv_new))


def _prep_weights(w_in, sb_bias, conv_w, a_log, dt_bias, gdn_norm_w, w_o_a, w_o_b, w_out,
                  ln1_g, ln1_b, w_gu, w_down, ln2_g, ln2_b):
    w = SB_HEADS * SB_HEAD_DIM
    d = w_in.shape[0]
    dff = w_down.shape[0]
    hh = GDN_HEADS
    o = 0
    cols = {}
    for nm, width in (("w_q", w), ("w_k", w), ("w_v", w), ("w_conv", 3 * w), ("w_z", w),
                      ("w_ba", 2 * hh), ("w_ga", d), ("w_gb", d)):
        cols[nm] = w_in[:, o:o + width].astype(BF16)
        o += width
    cols["w_ba"] = jnp.pad(cols["w_ba"], ((0, 0), (0, LANES - 2 * hh)))
    gp = jnp.zeros((8, LANES), F32)
    gp = gp.at[0, hh:2 * hh].set(-jnp.exp(a_log.astype(F32))).at[1, hh:2 * hh].set(dt_bias.astype(F32))
    cols.update(
        sb_bias=sb_bias, conv_w=conv_w.astype(F32), gparams=gp,
        norm_w=gdn_norm_w.astype(F32).reshape(1, GDN_DIM),
        w_o_a=w_o_a.astype(BF16), w_o_b=w_o_b.astype(BF16), w_out=w_out.astype(BF16),
        ln1_g=ln1_g.reshape(1, d), ln1_b=ln1_b.reshape(1, d),
        w_g=w_gu[:, :dff].astype(BF16), w_u=w_gu[:, dff:].astype(BF16), w_down=w_down.astype(BF16),
        ln2_g=ln2_g.reshape(1, d), ln2_b=ln2_b.reshape(1, d))
    return cols


def kernel(x_prompt, x_sample, cache_k, cache_v, state_gdn, state_conv, page_table, w_in, sb_bias, conv_w,
           a_log, dt_bias, gdn_norm_w, w_o_a, w_o_b, w_out, ln1_g, ln1_b, w_gu, w_down, ln2_g, ln2_b):
    depth = w_in.shape[0]
    alpha = (2.0 * depth) ** 0.25
    w = SB_HEADS * SB_HEAD_DIM
    nphys, psz = cache_k.shape[1], cache_k.shape[2]
    ck = cache_k.reshape(depth * nphys, psz * SB_HEADS, SB_HEAD_DIM)
    cv = cache_v.reshape(depth * nphys, psz * SB_HEADS, SB_HEAD_DIM)
    yp, ys = x_prompt, x_sample
    outs = [[] for _ in range(8)]
    for l in range(depth):
        wts = _prep_weights(w_in[l], sb_bias[l], conv_w[l], a_log[l], dt_bias[l], gdn_norm_w[l], w_o_a[l],
                            w_o_b[l], w_out[l], ln1_g[l], ln1_b[l], w_gu[l], w_down[l], ln2_g[l], ln2_b[l])
        nbp, tp = yp.shape[0], yp.shape[1]
        conv0 = jnp.zeros((nbp, CONV_WIDTH - 1, 3 * w), F32)
        s0 = jnp.zeros((nbp, GDN_HEADS, GDN_DIM, GDN_DIM), F32)

        def sb_p(q, k, v):
            return jnp.concatenate([_sb_prompt(q[b * tp:(b + 1) * tp], k[b * tp:(b + 1) * tp],
                                               v[b * tp:(b + 1) * tp], wts["sb_bias"]) for b in range(nbp)], axis=0)

        yp, (kp, vp, sp, cp) = _layer(yp, sb_p, conv0, s0, wts, alpha, "p", 1024)

        nbs, ts = ys.shape[0], ys.shape[1]
        page_ids = page_table + l * nphys

        def sb_s(q, k, v):
            r3 = lambda a: a.reshape(nbs, ts, w)
            return _sb_sample(r3(q), r3(k), r3(v), wts["sb_bias"], ck, cv, page_ids).reshape(nbs * ts, w)

        ys, (ksm, vsm, ssm, csm) = _layer(ys, sb_s, state_conv[l], state_gdn[l], wts, alpha, "s", 256)
        for lst, val in zip(outs, (kp, vp, sp, cp, ksm, vsm, ssm, csm)):
            lst.append(val)
    return (yp, ys) + tuple(jnp.stack(o) for o in outs)
```

```python
import functools

import jax
import jax.numpy as jnp
from jax import lax
from jax.experimental import pallas as pl
from jax.experimental.pallas import tpu as pltpu

F32 = jnp.float32
BF16 = jnp.bfloat16

LANES = 128
VMEM_LIMIT = 56 * 1024 * 1024

SB_HEADS = 8
SB_HEAD_DIM = 128
GDN_HEADS = 8
GDN_DIM = 128
CONV_WIDTH = 4
GDN_CHUNK = 64
LN_EPS = 1e-5
RMS_EPS = 1e-6
L2_EPS = 1e-6

SB_TQ = 256
SB_TK = 128
SB_HEAD_GROUP = 4
SB_KEY_BLOCKS = 16
SB_SKEW = 2
SB_PAGE_GROUP = 16
GDN_SPLIT_STEPS = 3
GDN_CHUNK_GROUP = 8
GDN_PREP_UNROLL = 4


def _params(*sem):
    return pltpu.CompilerParams(dimension_semantics=sem, vmem_limit_bytes=VMEM_LIMIT)


def _dot(a, b):
    return jnp.dot(a, b, preferred_element_type=F32)


def _dot_nt(a, b):
    return lax.dot_general(a, b, (((1,), (1,)), ((), ())), preferred_element_type=F32)


def _dot_tn(a, b):
    return lax.dot_general(a, b, (((0,), (0,)), ((), ())), preferred_element_type=F32)


def _split2(x):
    hi = x.astype(BF16)
    lo = (x - hi.astype(F32)).astype(BF16)
    return hi, lo


def _prefix_rows(x, rowid):
    d = 1
    while d < x.shape[0]:
        x = x + jnp.where(rowid >= d, pltpu.roll(x, d, 0), 0.0)
        d *= 2
    return x


def _neg_softplus(z):
    return jnp.minimum(-z, 0.0) - jnp.log(1.0 + jnp.exp(-jnp.abs(z)))


LOG2E = 1.4426950408889634


def _log2_one_minus_sigmoid(nz):
    nabs = pltpu.bitcast(pltpu.bitcast(nz, jnp.uint32) | jnp.uint32(0x80000000), F32)
    return jnp.minimum(nz, 0.0) - jnp.log(1.0 + jnp.exp2(nabs)) * LOG2E


def _softplus(z):
    return jnp.maximum(z, 0.0) + jnp.log1p(jnp.exp(-jnp.abs(z)))


def _layernorm(x, g, b):
    mu = jnp.mean(x, axis=-1, keepdims=True)
    xc = x - mu
    var = jnp.mean(xc * xc, axis=-1, keepdims=True)
    return xc * lax.rsqrt(var + LN_EPS) * g + b


def _mm_kernel(x_ref, w_ref, o_ref):
    o_ref[...] = _dot(x_ref[...], w_ref[...])


def _matmul(x, w, bm, bn, name):
    m, k = x.shape
    n = w.shape[1]
    bm, bn = min(bm, m), min(bn, n)
    return pl.pallas_call(
        _mm_kernel,
        grid=(n // bn, m // bm),
        in_specs=[pl.BlockSpec((bm, k), lambda j, i: (i, 0)),
                  pl.BlockSpec((k, bn), lambda j, i: (0, j))],
        out_specs=pl.BlockSpec((bm, bn), lambda j, i: (i, j)),
        out_shape=jax.ShapeDtypeStruct((m, n), F32),
        compiler_params=_params("parallel", "parallel"),
        name=name,
    )(x, w)


def _mm_dual_kernel(x_ref, w_ref, o_ref, ob_ref):
    r = _dot(x_ref[...], w_ref[...])
    o_ref[...] = r
    ob_ref[...] = r.astype(BF16)


def _mm_nt_kernel(wt_ref, x_ref, o_ref, *, scale):
    o_ref[...] = (_dot_nt(wt_ref[...], x_ref[...]) * scale).astype(BF16)


def _mm_vt_kernel(x_ref, w_ref, o_ref, vt_ref):
    r = _dot(x_ref[...], w_ref[...])
    o_ref[...] = r
    for h in range(vt_ref.shape[0]):
        for kb in range(vt_ref.shape[1]):
            tile = r[kb * SB_TK:(kb + 1) * SB_TK, h * SB_HEAD_DIM:(h + 1) * SB_HEAD_DIM]
            vt_ref[h, kb] = tile.T.astype(BF16)


def _attention_projections(x, wq_t, wk, wv, bm, tag):
    m, k = x.shape
    n = wk.shape[1]
    rows = pl.BlockSpec((bm, k), lambda i: (i, 0))
    wfix = pl.BlockSpec((k, n), lambda i: (0, 0))
    out = pl.BlockSpec((bm, n), lambda i: (i, 0))
    par = _params("parallel")
    qt = pl.pallas_call(
        functools.partial(_mm_nt_kernel, scale=-(SB_HEAD_DIM ** -0.5) * LOG2E),
        grid=(m // bm,),
        in_specs=[pl.BlockSpec((n, k), lambda i: (0, 0)), rows],
        out_specs=pl.BlockSpec((n, bm), lambda i: (0, i)),
        out_shape=jax.ShapeDtypeStruct((n, m), BF16),
        compiler_params=par, name=f"proj_qt_{tag}",
    )(wq_t, x)
    kf, kb = pl.pallas_call(
        _mm_dual_kernel,
        grid=(m // bm,),
        in_specs=[rows, wfix],
        out_specs=[out, out],
        out_shape=[jax.ShapeDtypeStruct((m, n), F32), jax.ShapeDtypeStruct((m, n), BF16)],
        compiler_params=par, name=f"proj_k_{tag}",
    )(x, wk)
    heads = n // SB_HEAD_DIM
    vf, vt = pl.pallas_call(
        _mm_vt_kernel,
        grid=(m // bm,),
        in_specs=[rows, wfix],
        out_specs=[out, pl.BlockSpec((heads, bm // SB_TK, SB_HEAD_DIM, SB_TK), lambda i: (0, i, 0, 0))],
        out_shape=[jax.ShapeDtypeStruct((m, n), F32),
                   jax.ShapeDtypeStruct((heads, m // SB_TK, SB_HEAD_DIM, SB_TK), BF16)],
        compiler_params=par, name=f"proj_v_{tag}",
    )(x, wv)
    return qt, kf, kb, vf, vt


def _gdn_pre_kernel(cur_ref, halo_ref, prev_ref, cw_ref, ba_ref, gp_ref, q_ref, k_ref, v_ref, bg_ref, ext_scr):
    j = pl.program_id(1)
    bm = cur_ref.shape[0]
    w = GDN_HEADS * GDN_DIM
    for blk in range(3 * GDN_HEADS):
        sl = slice(blk * GDN_DIM, (blk + 1) * GDN_DIM)
        cur = cur_ref[:, sl]
        ext_scr[0:8, :] = jnp.where(j == 0, prev_ref[:, sl], halo_ref[:, sl])
        ext_scr[8:, :] = cur
        conv = cur * cw_ref[CONV_WIDTH - 1:CONV_WIDTH, sl]
        for d in range(1, CONV_WIDTH):
            conv = conv + ext_scr[pl.ds(8 - d, bm), :] * cw_ref[CONV_WIDTH - 1 - d:CONV_WIDTH - d, sl]
        c = conv * jax.nn.sigmoid(conv)
        osl = slice(sl.start % w, sl.start % w + GDN_DIM)
        if blk < GDN_HEADS:
            q_ref[:, osl] = c * lax.rsqrt(jnp.sum(c * c, axis=-1, keepdims=True) + L2_EPS) * (GDN_DIM ** -0.5)
        elif blk < 2 * GDN_HEADS:
            k_ref[:, osl] = c * lax.rsqrt(jnp.sum(c * c, axis=-1, keepdims=True) + L2_EPS)
        else:
            v_ref[:, osl] = c
    ba = ba_ref[...]
    gp = gp_ref[...]
    lane = lax.broadcasted_iota(jnp.int32, ba.shape, 1)
    bg_ref[...] = jnp.where(lane < GDN_HEADS, jax.nn.sigmoid(ba), gp[0:1] * _softplus(ba + gp[1:2]))


def _gdn_pre(conv_in, conv_prev8, conv_w, ba, gparams, bm, name):
    nb, t, c3 = conv_in.shape
    w = c3 // 3
    bm = min(bm, t)
    hb = bm // 8
    row = lambda b, j: (b, j, 0)
    out = jax.ShapeDtypeStruct((nb, t, w), F32)
    return pl.pallas_call(
        _gdn_pre_kernel,
        grid=(nb, t // bm),
        in_specs=[pl.BlockSpec((None, bm, c3), row),
                  pl.BlockSpec((None, 8, c3), lambda b, j: (b, jnp.maximum(j * hb - 1, 0), 0)),
                  pl.BlockSpec((None, 8, c3), lambda b, j: (b, 0, 0)),
                  pl.BlockSpec((CONV_WIDTH, c3), lambda b, j: (0, 0)),
                  pl.BlockSpec((None, bm, LANES), row),
                  pl.BlockSpec((8, LANES), lambda b, j: (0, 0))],
        out_specs=[pl.BlockSpec((None, bm, w), row)] * 3 + [pl.BlockSpec((None, bm, LANES), row)],
        out_shape=[out] * 3 + [jax.ShapeDtypeStruct((nb, t, LANES), F32)],
        scratch_shapes=[pltpu.VMEM((bm + 8, GDN_DIM), F32)],
        compiler_params=_params("parallel", "parallel"),
        name=name,
    )(conv_in, conv_in, conv_prev8, conv_w, ba, gparams)


def _gdn_prep_kernel(q_ref, k_ref, v_ref, bg_ref, u_ref, w_ref, qd_ref, kd_ref, intra_ref, gl_ref,
                     *, chunks):
    n = GDN_CHUNK
    ri = lax.broadcasted_iota(jnp.int32, (n, n), 0)
    ci = lax.broadcasted_iota(jnp.int32, (n, n), 1)
    incl = ri >= ci
    strict = ri > ci
    lane_pad = jnp.zeros((n, GDN_DIM - n), F32)
    rowid = lax.broadcasted_iota(jnp.int32, (n, GDN_DIM), 0)
    unroll = GDN_PREP_UNROLL if chunks % GDN_PREP_UNROLL == 0 else 1

    def chunk(cc, carry):
        cs = [cc * unroll + d for d in range(unroll)]
        hs = range(GDN_HEADS * unroll)
        sls = [slice((h % GDN_HEADS) * GDN_DIM, (h % GDN_HEADS + 1) * GDN_DIM) for h in hs]
        rws = [pl.ds(pl.multiple_of(cs[h // GDN_HEADS] * n, n), n) for h in hs]
        ks = [k_ref[r, sl] for r, sl in zip(rws, sls)]
        bgs = [bg_ref[r, :] for r in rws[::GDN_HEADS]]
        lanes = lambda col: jnp.broadcast_to(col, (n, GDN_DIM))
        betas = [lanes(bgs[h // GDN_HEADS][:, h % GDN_HEADS:h % GDN_HEADS + 1]) for h in hs]
        gs = [lanes(bgs[h // GDN_HEADS][:, GDN_HEADS + h % GDN_HEADS:GDN_HEADS + h % GDN_HEADS + 1]) for h in hs]
        gcums = [_prefix_rows(g, rowid) for g in gs]
        grows = [jnp.broadcast_to(gc.T[0:1, :], (n, n)) for gc in gcums]
        kbfs = [k.astype(BF16) for k in ks]
        kbs = [k * beta for k, beta in zip(ks, betas)]
        kks = [_dot_nt(kb.astype(BF16), kbf) for kb, kbf in zip(kbs, kbfs)]
        qks = [_dot_nt(q_ref[r, sl].astype(BF16), kbf) for r, sl, kbf in zip(rws, sls, kbfs)]
        gams = [jnp.where(incl, jnp.exp(jnp.where(incl, gc[:, :n] - gr, 0.0)), 0.0) for gc, gr in zip(gcums, grows)]
        egs = [jnp.exp(gc) for gc in gcums]
        mps = [-jnp.where(strict, kk * gam, 0.0) for kk, gam in zip(kks, gams)]
        sols = [jnp.concatenate([v_ref[r, sl] * beta, kb * eg], axis=-1)
                for r, sl, beta, kb, eg in zip(rws, sls, betas, kbs, egs)]
        span = 1
        while span < n:
            mpbs = [mp.astype(BF16) for mp in mps]
            if span < (1 << GDN_SPLIT_STEPS):
                upds = [_dot(mpb, jnp.concatenate(_split2(sol), axis=-1)) for mpb, sol in zip(mpbs, sols)]
                sols = [sol + (upd[:, :2 * GDN_DIM] + upd[:, 2 * GDN_DIM:]) for sol, upd in zip(sols, upds)]
            else:
                sols = [sol + _dot(mpb, sol.astype(BF16)) for mpb, sol in zip(mpbs, sols)]
            span *= 2
            if span < n:
                mps = [_dot(mpb, mpb) for mpb in mpbs]
        for h in hs:
            sl, rows, head = sls[h], rws[h], h % GDN_HEADS
            glast = gcums[h][n - 1:n, :]
            u_ref[rows, sl] = sols[h][:, :GDN_DIM]
            w_ref[rows, sl] = sols[h][:, GDN_DIM:].astype(BF16)
            intra = jnp.where(incl, qks[h] * gams[h], 0.0)
            intra_ref[rows, sl] = jnp.concatenate([intra, lane_pad], axis=-1).astype(BF16)
            qd_ref[rows, sl] = (q_ref[rows, sl] * egs[h]).astype(BF16)
            kd_ref[rows, sl] = (ks[h] * jnp.exp(glast - gcums[h])).astype(BF16)
            gl_ref[cs[h // GDN_HEADS], head:head + 1, :] = jnp.exp(glast)
        return carry

    lax.fori_loop(0, chunks // unroll, chunk, 0)


def _gdn_scan_kernel(u_ref, w_ref, qd_ref, kd_ref, intra_ref, gl_ref, s0_ref, o_ref, sfin_ref, s_scr, *, chunks):
    n = GDN_CHUNK
    cg = pl.program_id(1)

    @pl.when(cg == 0)
    def _():
        s_scr[...] = s0_ref[...]

    def chunk(c, carry):
        rows = pl.ds(pl.multiple_of(c * n, n), n)
        hs = range(GDN_HEADS)
        sls = [slice(h * GDN_DIM, (h + 1) * GDN_DIM) for h in hs]
        ss = [s_scr[h] for h in hs]
        rs = [_dot(jnp.concatenate([w_ref[rows, sl], qd_ref[rows, sl]], axis=0), s.astype(BF16))
              for sl, s in zip(sls, ss)]
        vbs = [(u_ref[rows, sl] - r[:n]).astype(BF16) for sl, r in zip(sls, rs)]
        upds = [_dot_tn(kd_ref[rows, sl], vb) for sl, vb in zip(sls, vbs)]
        for h in hs:
            s_scr[h] = ss[h] * gl_ref[c, h:h + 1, :] + upds[h]
        for h in hs:
            o_ref[rows, sls[h]] = rs[h][n:] + _dot(intra_ref[rows, h * GDN_DIM:h * GDN_DIM + n], vbs[h])
        return carry

    lax.fori_loop(0, chunks, chunk, 0)

    @pl.when(cg == pl.num_programs(1) - 1)
    def _():
        sfin_ref[...] = s_scr[...]


def _gdn(q, k, v, bg, s0, tag):
    nb, t, w = q.shape
    nc = t // GDN_CHUNK
    cgs = min(GDN_CHUNK_GROUP, nc)
    rows = cgs * GDN_CHUNK
    tok = pl.BlockSpec((None, rows, w), lambda b, c: (b, c, 0))
    gl = pl.BlockSpec((None, cgs, GDN_HEADS, GDN_DIM), lambda b, c: (b, c, 0, 0))
    st = pl.BlockSpec((None, GDN_HEADS, GDN_DIM, GDN_DIM), lambda b, c: (b, 0, 0, 0))
    tok_f32 = jax.ShapeDtypeStruct((nb, t, w), F32)
    tok_bf16 = jax.ShapeDtypeStruct((nb, t, w), BF16)
    gl_shape = jax.ShapeDtypeStruct((nb, nc, GDN_HEADS, GDN_DIM), F32)
    u, wy, qd, kd, intra, glast = pl.pallas_call(
        functools.partial(_gdn_prep_kernel, chunks=cgs),
        grid=(nb, nc // cgs),
        in_specs=[tok] * 3 + [pl.BlockSpec((None, rows, LANES), lambda b, c: (b, c, 0))],
        out_specs=[tok] * 5 + [gl],
        out_shape=[tok_f32] + [tok_bf16] * 4 + [gl_shape],
        compiler_params=_params("parallel", "parallel"),
        name=f"gdn_prep_{tag}",
    )(q, k, v, bg)
    return pl.pallas_call(
        functools.partial(_gdn_scan_kernel, chunks=cgs),
        grid=(nb, nc // cgs),
        in_specs=[tok] * 5 + [gl, st],
        out_specs=[tok, st],
        out_shape=[tok_f32, jax.ShapeDtypeStruct((nb, GDN_HEADS, GDN_DIM, GDN_DIM), F32)],
        scratch_shapes=[pltpu.VMEM((GDN_HEADS, GDN_DIM, GDN_DIM), F32)],
        compiler_params=_params("parallel", "arbitrary"),
        name=f"gdn_scan_{tag}",
    )(u, wy, qd, kd, intra, glast, s0)


def _sbp_kernel(bias_ref, qt_ref, k_ref, vt_ref, u2_ref, o_ref, acc_ref):
    hg = pl.program_id(0)
    i = pl.program_id(1)
    nh = qt_ref.shape[0]
    u2 = u2_ref[...]
    tk, tq = SB_TK, SB_TQ

    def blocks(js, accs, offs):
        valid = [None if o is None else
                 (lax.broadcasted_iota(jnp.int32, (tk, tq), 0) + o) < lax.broadcasted_iota(jnp.int32, (tk, tq), 1)
                 for o in offs]
        chains = [(b, h) for b in range(len(js)) for h in range(nh)]
        rows = [pl.ds(pl.multiple_of(j * tk, tk), tk) for j in js]
        accs = list(accs)
        nz, lg, rinc = {}, {}, {}

        def st_qk(c):
            b, h = c
            nz[c] = _dot(k_ref[rows[b], h * SB_HEAD_DIM:(h + 1) * SB_HEAD_DIM], qt_ref[h]) + bias_ref[hg * nh + h]

        def st_lg(c):
            v = _log2_one_minus_sigmoid(nz[c])
            lg[c] = v if valid[c[0]] is None else jnp.where(valid[c[0]], v, 0.0)

        def st_sum(c):
            rinc[c] = _dot(u2, jnp.concatenate(_split2(lg.pop(c)), axis=0))

        def st_av(c):
            b, h = c
            a = jnp.exp2((rinc[c] + accs[h]) - nz.pop(c))
            if valid[b] is not None:
                a = jnp.where(valid[b], a, 0.0)
            acc_ref[h] += _dot(vt_ref[h, js[b]], a.astype(BF16))
            accs[h] = accs[h] + rinc.pop(c)[0:1, :]

        stages = [st_qk, st_lg, st_sum, st_av]
        for step in range(len(chains) + (len(stages) - 1) * SB_SKEW):
            for si, st in enumerate(stages):
                ci = step - si * SB_SKEW
                if 0 <= ci < len(chains):
                    st(chains[ci])
        return tuple(accs)

    acc_ref[...] = jnp.zeros_like(acc_ref)
    per = tq // tk
    nb = SB_KEY_BLOCKS
    first = i * per
    accs = blocks([first + per - 1 - d for d in range(per)], tuple(jnp.zeros((1, tq), F32) for _ in range(nh)),
                  [(per - 1 - d) * tk for d in range(per)])
    accs = lax.fori_loop(0, first // nb,
                         lambda n, c: blocks([first - 1 - nb * n - d for d in range(nb)], c, [None] * nb), accs)
    done = (first // nb) * nb
    size = nb // 2
    while size >= per:
        take = ((first - done) // size) % 2
        start = first - done - 1
        accs = lax.fori_loop(0, take, lambda n, c, start=start, size=size:
                             blocks([start - d for d in range(size)], c, [None] * size), accs)
        done = done + take * size
        size //= 2
    for h in range(nh):
        o_ref[:, h * SB_HEAD_DIM:(h + 1) * SB_HEAD_DIM] = acc_ref[h].T.astype(BF16)


def _sb_prompt(qt, kb, vt, bias):
    hh, dh, s = qt.shape
    hg = SB_HEAD_GROUP
    nkb = s // SB_TK
    r = jnp.arange(SB_TK)
    u = (r[None, :] >= r[:, None]).astype(BF16)
    u2 = jnp.concatenate([u, u], axis=1)
    once = pl.Buffered(1)
    return pl.pallas_call(
        _sbp_kernel,
        grid=(hh // hg, s // SB_TQ),
        in_specs=[pl.BlockSpec(memory_space=pltpu.SMEM),
                  pl.BlockSpec((hg, dh, SB_TQ), lambda g, i: (g, 0, i)),
                  pl.BlockSpec((s, hg * dh), lambda g, i: (0, g), pipeline_mode=once),
                  pl.BlockSpec((hg, nkb, dh, SB_TK), lambda g, i: (g, 0, 0, 0), pipeline_mode=once),
                  pl.BlockSpec((SB_TK, 2 * SB_TK), lambda g, i: (0, 0), pipeline_mode=once)],
        out_specs=pl.BlockSpec((SB_TQ, hg * dh), lambda g, i: (i, g)),
        out_shape=jax.ShapeDtypeStruct((s, hh * dh), BF16),
        scratch_shapes=[pltpu.VMEM((hg, dh, SB_TQ), F32)],
        compiler_params=_params("parallel", "arbitrary"),
        name="sb_prompt",
    )(bias.astype(F32) * (-LOG2E), qt, kb, vt, u2)


def _sbs_kernel(pt_ref, qbd_ref, bias_ref, knew_ref, vnew_ref, *rest, pages):
    kc_refs, vc_refs = rest[:pages], rest[pages:2 * pages]
    u2_ref, o_ref, acc_scr, out_scr = rest[2 * pages:]
    p = pl.program_id(1)
    qbd = qbd_ref[...]
    bias = bias_ref[...]
    u2 = u2_ref[...]
    nrow, npos = bias.shape
    t = nrow // SB_HEADS

    def page_rows(ref):
        if ref.shape[1] != SB_HEAD_DIM:
            return ref[...].astype(BF16)
        return jnp.concatenate([ref[pl.ds(h, npos, stride=SB_HEADS), :].astype(BF16) for h in range(SB_HEADS)],
                               axis=1)

    def group(k_refs, v_refs, masked):
        kcat = jnp.concatenate([page_rows(r) for r in k_refs], axis=0)
        vcat = jnp.concatenate([page_rows(r) for r in v_refs], axis=0)
        zall = _dot_nt(qbd, kcat)
        acc = acc_scr[...]
        probs = []
        for gi in range(len(k_refs)):
            z = zall[:, gi * npos:(gi + 1) * npos] + bias
            lg = _neg_softplus(z)
            if masked:
                tok = lax.broadcasted_iota(jnp.int32, (nrow, npos), 0) % t
                valid = lax.broadcasted_iota(jnp.int32, (nrow, npos), 1) < tok
                lg = jnp.where(valid, lg, 0.0)
            hi, lo = _split2(lg)
            rinc = _dot(jnp.concatenate([hi, lo], axis=1), u2)
            a = jnp.exp(z + rinc + acc)
            if masked:
                a = jnp.where(valid, a, 0.0)
            probs.append(a.astype(BF16))
            acc = acc + jnp.sum(lg, axis=-1, keepdims=True)
        out_scr[...] += _dot(jnp.concatenate(probs, axis=1), vcat)
        acc_scr[...] = acc

    @pl.when(p == 0)
    def _():
        acc_scr[...] = jnp.zeros_like(acc_scr)
        out_scr[...] = jnp.zeros_like(out_scr)
        group([knew_ref], [vnew_ref], True)

    group(kc_refs, vc_refs, False)

    @pl.when(p == pl.num_programs(1) - 1)
    def _():
        for h in range(SB_HEADS):
            sl = slice(h * SB_HEAD_DIM, (h + 1) * SB_HEAD_DIM)
            o_ref[:, sl] = out_scr[h * t:(h + 1) * t, sl]


def _sb_sample(q, k, v, bias, cache_k, cache_v, page_ids):
    nb, t, w = q.shape
    hh, dh = SB_HEADS, SB_HEAD_DIM
    npg = page_ids.shape[1]
    psz = cache_k.shape[1] // hh
    pgs = min(SB_PAGE_GROUP, npg)
    qs = (q * (dh ** -0.5)).astype(BF16).reshape(nb, t, hh, dh)
    eye = jnp.eye(hh, dtype=BF16)
    qbd = (qs.transpose(0, 2, 1, 3)[:, :, :, None, :] * eye[None, :, None, :, None]).reshape(nb, hh * t, w)
    bias_rep = jnp.broadcast_to(jnp.repeat(bias.astype(F32), t)[:, None], (hh * t, psz))
    pad = ((0, 0), (0, psz - t), (0, 0))
    knew, vnew = jnp.pad(k, pad), jnp.pad(v, pad)
    r = jnp.arange(psz)
    u = (r[:, None] >= r[None, :]).astype(BF16)
    u2 = jnp.concatenate([u, u], axis=0)
    per_seq = lambda b, p, pt: (b, 0, 0)
    cache = [pl.BlockSpec((None, psz * hh, dh), lambda b, p, pt, gi=gi: (pt[b, npg - 1 - (p * pgs + gi)], 0, 0))
             for gi in range(pgs)]
    return pl.pallas_call(
        functools.partial(_sbs_kernel, pages=pgs),
        grid_spec=pltpu.PrefetchScalarGridSpec(
            num_scalar_prefetch=1,
            grid=(nb, npg // pgs),
            in_specs=[pl.BlockSpec((None, hh * t, w), per_seq),
                      pl.BlockSpec((hh * t, psz), lambda b, p, pt: (0, 0)),
                      pl.BlockSpec((None, psz, w), per_seq),
                      pl.BlockSpec((None, psz, w), per_seq)]
                     + cache + cache
                     + [pl.BlockSpec((2 * psz, psz), lambda b, p, pt: (0, 0))],
            out_specs=pl.BlockSpec((None, t, w), per_seq),
            scratch_shapes=[pltpu.VMEM((hh * t, psz), F32), pltpu.VMEM((hh * t, w), F32)],
        ),
        out_shape=jax.ShapeDtypeStruct((nb, t, w), F32),
        compiler_params=_params("parallel", "arbitrary"),
        name="sb_sample",
    )(page_ids, qbd, bias_rep, knew, vnew, *([cache_k] * pgs), *([cache_v] * pgs), u2)


def _merge_kernel(oa_ref, ob_ref, zg_ref, ga_ref, gb_ref, x_ref, nw_ref, woa_ref, wob_ref, wout_ref,
                  g1_ref, b1_ref, h_ref, hb_ref, *, alpha):
    ob = ob_ref[...]
    zg = zg_ref[...]
    nw = nw_ref[...]
    parts = []
    for h in range(GDN_HEADS):
        sl = slice(h * GDN_DIM, (h + 1) * GDN_DIM)
        oh = ob[:, sl]
        zh = zg[:, sl]
        rn = oh * lax.rsqrt(jnp.mean(oh * oh, axis=-1, keepdims=True) + RMS_EPS)
        parts.append((rn * nw * (zh * jax.nn.sigmoid(zh))).astype(BF16))
    obn = jnp.concatenate(parts, axis=-1)
    merged = (jax.nn.sigmoid(ga_ref[...]) * _dot(oa_ref[...], woa_ref[...])
              + jax.nn.sigmoid(gb_ref[...]) * _dot(obn, wob_ref[...]))
    y = alpha * x_ref[...] + _dot(merged.astype(BF16), wout_ref[...])
    hn = _layernorm(y, g1_ref[...], b1_ref[...])
    h_ref[...] = hn
    hb_ref[...] = hn.astype(BF16)


def _merge(oa, ob, zg, ga, gb, x, nw, woa, wob, wout, g1, b1, alpha, bm, name):
    m, d = x.shape
    w = oa.shape[1]
    bm = min(bm, m)
    row = lambda i: (i, 0)
    fix = lambda i: (0, 0)
    once = functools.partial(pl.BlockSpec, index_map=fix, pipeline_mode=pl.Buffered(1))
    return pl.pallas_call(
        functools.partial(_merge_kernel, alpha=alpha),
        grid=(m // bm,),
        in_specs=[pl.BlockSpec((bm, w), row), pl.BlockSpec((bm, w), row), pl.BlockSpec((bm, w), row),
                  pl.BlockSpec((bm, d), row), pl.BlockSpec((bm, d), row), pl.BlockSpec((bm, d), row),
                  pl.BlockSpec((1, GDN_DIM), fix),
                  once((w, d)), once((w, d)), once((d, d)),
                  pl.BlockSpec((1, d), fix), pl.BlockSpec((1, d), fix)],
        out_specs=[pl.BlockSpec((bm, d), row), pl.BlockSpec((bm, d), row)],
        out_shape=[jax.ShapeDtypeStruct((m, d), F32), jax.ShapeDtypeStruct((m, d), BF16)],
        compiler_params=_params("parallel"),
        name=name,
    )(oa, ob, zg, ga, gb, x, nw, woa, wob, wout, g1, b1)


def _ffn_kernel(hb_ref, h_ref, wg_ref, wu_ref, wd_ref, g2_ref, b2_ref, y_ref, acc_ref, *, alpha):
    f = pl.program_id(1)
    hb = hb_ref[...]
    gate = _dot(hb, wg_ref[...])
    up = _dot(hb, wu_ref[...])
    act = (gate * jax.nn.sigmoid(gate) * up).astype(BF16)
    part = _dot(act, wd_ref[...])

    @pl.when(f == 0)
    def _():
        acc_ref[...] = part

    @pl.when(f != 0)
    def _():
        acc_ref[...] += part

    @pl.when(f == pl.num_programs(1) - 1)
    def _():
        y_ref[...] = _layernorm(alpha * h_ref[...] + acc_ref[...], g2_ref[...], b2_ref[...])


def _ffn(hb, h, wg, wu, wd, g2, b2, alpha, bm, bf, name):
    m, d = h.shape
    dff = wg.shape[1]
    bm = min(bm, m)
    row = lambda i, f: (i, 0)
    fix = lambda i, f: (0, 0)
    return pl.pallas_call(
        functools.partial(_ffn_kernel, alpha=alpha),
        grid=(m // bm, dff // bf),
        in_specs=[pl.BlockSpec((bm, d), row), pl.BlockSpec((bm, d), row),
                  pl.BlockSpec((d, bf), lambda i, f: (0, f)), pl.BlockSpec((d, bf), lambda i, f: (0, f)),
                  pl.BlockSpec((bf, d), lambda i, f: (f, 0)),
                  pl.BlockSpec((1, d), fix), pl.BlockSpec((1, d), fix)],
        out_specs=pl.BlockSpec((bm, d), row),
        out_shape=jax.ShapeDtypeStruct((m, d), F32),
        scratch_shapes=[pltpu.VMEM((bm, d), F32)],
        compiler_params=_params("parallel", "arbitrary"),
        name=name,
    )(hb, h, wg, wu, wd, g2, b2)


def _layer(x, sb_mix, conv_prev, s_prev, wts, alpha, tag, bm):
    nb, t, d = x.shape
    m = nb * t
    w = SB_HEADS * SB_HEAD_DIM
    xf = x.reshape(m, d)
    xb = xf.astype(BF16)
    mm = lambda wt, nm: _matmul(xb, wt, bm, 1024, f"proj_{nm}_{tag}")
    o_a, ka, va = sb_mix(xb, mm)
    conv_in = mm(wts["w_conv"], "conv").reshape(nb, t, 3 * w)
    zg, ga, gb, ba = mm(wts["w_z"], "z"), mm(wts["w_ga"], "ga"), mm(wts["w_gb"], "gb"), mm(wts["w_ba"], "ba")

    prev8 = jnp.pad(conv_prev.astype(F32), ((0, 0), (8 - (CONV_WIDTH - 1), 0), (0, 0)))
    qn, kn, vv, bg = _gdn_pre(conv_in, prev8, wts["conv_w"], ba.reshape(nb, t, LANES), wts["gparams"],
                                   256, f"gdn_pre_{tag}")
    tp = -(-t // GDN_CHUNK) * GDN_CHUNK
    if tp != t:
        padt = lambda a: jnp.pad(a, ((0, 0), (0, tp - t), (0, 0)))
        qn, kn, vv, bg = map(padt, (qn, kn, vv, bg))
    o_b, s_new = _gdn(qn, kn, vv, bg, s_prev.astype(F32), tag)
    o_b = o_b[:, :t].reshape(m, w)

    h, hb = _merge(o_a, o_b, zg, ga, gb, xf, wts["norm_w"], wts["w_o_a"], wts["w_o_b"], wts["w_out"],
                   wts["ln1_g"], wts["ln1_b"], alpha, 256, f"merge_{tag}")
    y = _ffn(hb, h, wts["w_g"], wts["w_u"], wts["w_down"], wts["ln2_g"], wts["ln2_b"], alpha, 512, 512,
             f"ffn_{tag}")
    conv_new = jnp.concatenate([conv_prev.astype(F32), conv_in], axis=1)[:, -(CONV_WIDTH - 1):]
    return (y.reshape(nb, t, d),
            (ka.reshape(nb, t, SB_HEADS, SB_HEAD_DIM), va.reshape(nb, t, SB_HEADS, SB_HEAD_DIM), s_new, conv_new))


def _prep_weights(w_in, sb_bias, conv_w, a_log, dt_bias, gdn_norm_w, w_o_a, w_o_b, w_out,
                  ln1_g, ln1_b, w_gu, w_down, ln2_g, ln2_b):
    w = SB_HEADS * SB_HEAD_DIM
    d = w_in.shape[0]
    dff = w_down.shape[0]
    hh = GDN_HEADS
    o = 0
    cols = {}
    for nm, width in (("w_q", w), ("w_k", w), ("w_v", w), ("w_conv", 3 * w), ("w_z", w),
                      ("w_ba", 2 * hh), ("w_ga", d), ("w_gb", d)):
        cols[nm] = w_in[:, o:o + width].astype(BF16)
        o += width
    cols["w_ba"] = jnp.pad(cols["w_ba"], ((0, 0), (0, LANES - 2 * hh)))
    gp = jnp.zeros((8, LANES), F32)
    gp = gp.at[0, hh:2 * hh].set(-jnp.exp(a_log.astype(F32))).at[1, hh:2 * hh].set(dt_bias.astype(F32))
    cols.update(
        sb_bias=sb_bias, conv_w=conv_w.astype(F32), gparams=gp,
        norm_w=gdn_norm_w.astype(F32).reshape(1, GDN_DIM),
        w_o_a=w_o_a.astype(BF16), w_o_b=w_o_b.astype(BF16), w_out=w_out.astype(BF16),
        ln1_g=ln1_g.reshape(1, d), ln1_b=ln1_b.reshape(1, d),
        w_g=w_gu[:, :dff].astype(BF16), w_u=w_gu[:, dff:].astype(BF16), w_down=w_down.astype(BF16),
        ln2_g=ln2_g.reshape(1, d), ln2_b=ln2_b.reshape(1, d))
    return cols


def kernel(x_prompt, x_sample, cache_k, cache_v, state_gdn, state_conv, page_table, w_in, sb_bias, conv_w,
           a_log, dt_bias, gdn_norm_w, w_o_a, w_o_b, w_out, ln1_g, ln1_b, w_gu, w_down, ln2_g, ln2_b):
    depth = w_in.shape[0]
    alpha = (2.0 * depth) ** 0.25
    w = SB_HEADS * SB_HEAD_DIM
    nphys, psz = cache_k.shape[1], cache_k.shape[2]
    ck = cache_k.reshape(depth * nphys, psz * SB_HEADS, SB_HEAD_DIM)
    cv = cache_v.reshape(depth * nphys, psz * SB_HEADS, SB_HEAD_DIM)
    yp, ys = x_prompt, x_sample
    outs = [[] for _ in range(8)]
    for l in range(depth):
        wts = _prep_weights(w_in[l], sb_bias[l], conv_w[l], a_log[l], dt_bias[l], gdn_norm_w[l], w_o_a[l],
                            w_o_b[l], w_out[l], ln1_g[l], ln1_b[l], w_gu[l], w_down[l], ln2_g[l], ln2_b[l])
        nbp, tp = yp.shape[0], yp.shape[1]
        conv0 = jnp.zeros((nbp, CONV_WIDTH - 1, 3 * w), F32)
        s0 = jnp.zeros((nbp, GDN_HEADS, GDN_DIM, GDN_DIM), F32)

        assert nbp == 1, "prompt attention treats all prompt rows as one sequence"

        def sb_p(xb, project):
            qt, kf, kb, vf, vt = _attention_projections(xb, wts["w_q"].T, wts["w_k"], wts["w_v"], 1024, "p")
            return _sb_prompt(qt.reshape(SB_HEADS, SB_HEAD_DIM, tp), kb, vt, wts["sb_bias"]), kf, vf

        yp, (kp, vp, sp, cp) = _layer(yp, sb_p, conv0, s0, wts, alpha, "p", 1024)

        nbs, ts = ys.shape[0], ys.shape[1]
        page_ids = page_table + l * nphys

        def sb_s(xb, project):
            q, k, v = project(wts["w_q"], "q"), project(wts["w_k"], "k"), project(wts["w_v"], "v")
            r3 = lambda a: a.reshape(nbs, ts, w)
            o = _sb_sample(r3(q), r3(k), r3(v), wts["sb_bias"], ck, cv, page_ids)
            return o.reshape(nbs * ts, w).astype(BF16), k, v

        ys, (ksm, vsm, ssm, csm) = _layer(ys, sb_s, state_conv[l], state_gdn[l], wts, alpha, "s", 256)
        for lst, val in zip(outs, (kp, vp, sp, cp, ksm, vsm, ssm, csm)):
            lst.append(val)
    return (yp, ys) + tuple(jnp.stack(o) for o in outs)
```

```python
import functools

import jax
import jax.numpy as jnp
from jax import lax
from jax.experimental import pallas as pl
from jax.experimental.pallas import tpu as pltpu

F32 = jnp.float32
BF16 = jnp.bfloat16

LANES = 128
VMEM_LIMIT = 56 * 1024 * 1024

SB_HEADS = 8
SB_HEAD_DIM = 128
GDN_HEADS = 8
GDN_DIM = 128
CONV_WIDTH = 4
GDN_CHUNK = 64
LN_EPS = 1e-5
RMS_EPS = 1e-6
L2_EPS = 1e-6

SB_TQ = 256
SB_TK = 128
SB_HEAD_GROUP = 4
SB_KEY_BLOCKS = 16
SB_SKEW = 2
SB_PAGE_GROUP = 16
GDN_SPLIT_STEPS = 3
GDN_CHUNK_GROUP = 8
GDN_PREP_UNROLL = 4


def _params(*sem):
    return pltpu.CompilerParams(dimension_semantics=sem, vmem_limit_bytes=VMEM_LIMIT)


def _dot(a, b):
    return jnp.dot(a, b, preferred_element_type=F32)


def _dot_nt(a, b):
    return lax.dot_general(a, b, (((1,), (1,)), ((), ())), preferred_element_type=F32)


def _dot_tn(a, b):
    return lax.dot_general(a, b, (((0,), (0,)), ((), ())), preferred_element_type=F32)


def _split2(x):
    hi = x.astype(BF16)
    lo = (x - hi.astype(F32)).astype(BF16)
    return hi, lo


def _prefix_rows(x, rowid):
    d = 1
    while d < x.shape[0]:
        x = x + jnp.where(rowid >= d, pltpu.roll(x, d, 0), 0.0)
        d *= 2
    return x


def _neg_softplus(z):
    return jnp.minimum(-z, 0.0) - jnp.log(1.0 + jnp.exp(-jnp.abs(z)))


LOG2E = 1.4426950408889634


def _log2_one_minus_sigmoid(nz):
    nabs = pltpu.bitcast(pltpu.bitcast(nz, jnp.uint32) | jnp.uint32(0x80000000), F32)
    return jnp.minimum(nz, 0.0) - jnp.log(1.0 + jnp.exp2(nabs)) * LOG2E


def _softplus(z):
    return jnp.maximum(z, 0.0) + jnp.log1p(jnp.exp(-jnp.abs(z)))


def _layernorm(x, g, b):
    mu = jnp.mean(x, axis=-1, keepdims=True)
    xc = x - mu
    var = jnp.mean(xc * xc, axis=-1, keepdims=True)
    return xc * lax.rsqrt(var + LN_EPS) * g + b


def _mm_kernel(x_ref, w_ref, o_ref):
    o_ref[...] = _dot(x_ref[...], w_ref[...])


def _matmul(x, w, bm, bn, name, col0=0, n=None):
    m, k = x.shape
    n = w.shape[1] if n is None else n
    bm, bn = min(bm, m), min(bn, n)
    assert col0 % bn == 0 and n % bn == 0 and col0 + n <= w.shape[1]
    first = col0 // bn
    return pl.pallas_call(
        _mm_kernel,
        grid=(n // bn, m // bm),
        in_specs=[pl.BlockSpec((bm, k), lambda j, i: (i, 0)),
                  pl.BlockSpec((k, bn), lambda j, i: (0, first + j))],
        out_specs=pl.BlockSpec((bm, bn), lambda j, i: (i, j)),
        out_shape=jax.ShapeDtypeStruct((m, n), F32),
        compiler_params=_params("parallel", "parallel"),
        name=name,
    )(x, w)


def _mm_dual_kernel(x_ref, w_ref, o_ref, ob_ref):
    r = _dot(x_ref[...], w_ref[...])
    o_ref[...] = r
    ob_ref[...] = r.astype(BF16)


def _mm_nt_kernel(wt_ref, x_ref, o_ref, xb_ref, *, scale):
    xb = x_ref[...].astype(BF16)
    xb_ref[...] = xb
    o_ref[...] = (_dot_nt(wt_ref[...], xb) * scale).astype(BF16)


def _mm_vt_kernel(x_ref, w_ref, o_ref, vt_ref):
    r = _dot(x_ref[...], w_ref[...])
    o_ref[...] = r
    for h in range(vt_ref.shape[0]):
        for kb in range(vt_ref.shape[1]):
            tile = r[kb * SB_TK:(kb + 1) * SB_TK, h * SB_HEAD_DIM:(h + 1) * SB_HEAD_DIM]
            vt_ref[h, kb] = tile.T.astype(BF16)


def _attention_projections(x, wq_t, w, k_col, v_col, bm, tag):
    m, k = x.shape
    n = wq_t.shape[0]
    rows = pl.BlockSpec((bm, k), lambda i: (i, 0))
    out = pl.BlockSpec((bm, n), lambda i: (i, 0))
    par = _params("parallel")
    qt, xb = pl.pallas_call(
        functools.partial(_mm_nt_kernel, scale=-(SB_HEAD_DIM ** -0.5) * LOG2E),
        grid=(m // bm,),
        in_specs=[pl.BlockSpec((n, k), lambda i: (0, 0)), rows],
        out_specs=[pl.BlockSpec((n, bm), lambda i: (0, i)), rows],
        out_shape=[jax.ShapeDtypeStruct((n, m), BF16), jax.ShapeDtypeStruct((m, k), BF16)],
        compiler_params=par, name=f"proj_qt_{tag}",
    )(wq_t, x)
    kf, kb = pl.pallas_call(
        _mm_dual_kernel,
        grid=(m // bm,),
        in_specs=[rows, pl.BlockSpec((k, n), lambda i: (0, k_col))],
        out_specs=[out, out],
        out_shape=[jax.ShapeDtypeStruct((m, n), F32), jax.ShapeDtypeStruct((m, n), BF16)],
        compiler_params=par, name=f"proj_k_{tag}",
    )(xb, w)
    heads = n // SB_HEAD_DIM
    vf, vt = pl.pallas_call(
        _mm_vt_kernel,
        grid=(m // bm,),
        in_specs=[rows, pl.BlockSpec((k, n), lambda i: (0, v_col))],
        out_specs=[out, pl.BlockSpec((heads, bm // SB_TK, SB_HEAD_DIM, SB_TK), lambda i: (0, i, 0, 0))],
        out_shape=[jax.ShapeDtypeStruct((m, n), F32),
                   jax.ShapeDtypeStruct((heads, m // SB_TK, SB_HEAD_DIM, SB_TK), BF16)],
        compiler_params=par, name=f"proj_v_{tag}",
    )(xb, w)
    return xb, qt, kf, kb, vf, vt


def _gdn_pre_kernel(cur_ref, halo_ref, prev_ref, cw_ref, ba_ref, gp_ref, q_ref, k_ref, v_ref, bg_ref, ext_scr):
    j = pl.program_id(1)
    bm = cur_ref.shape[0]
    w = GDN_HEADS * GDN_DIM
    for blk in range(3 * GDN_HEADS):
        sl = slice(blk * GDN_DIM, (blk + 1) * GDN_DIM)
        cur = cur_ref[:, sl]
        ext_scr[0:8, :] = jnp.where(j == 0, prev_ref[:, sl], halo_ref[:, sl])
        ext_scr[8:, :] = cur
        conv = cur * cw_ref[CONV_WIDTH - 1:CONV_WIDTH, sl]
        for d in range(1, CONV_WIDTH):
            conv = conv + ext_scr[pl.ds(8 - d, bm), :] * cw_ref[CONV_WIDTH - 1 - d:CONV_WIDTH - d, sl]
        c = conv * jax.nn.sigmoid(conv)
        osl = slice(sl.start % w, sl.start % w + GDN_DIM)
        if blk < GDN_HEADS:
            q_ref[:, osl] = c * lax.rsqrt(jnp.sum(c * c, axis=-1, keepdims=True) + L2_EPS) * (GDN_DIM ** -0.5)
        elif blk < 2 * GDN_HEADS:
            k_ref[:, osl] = c * lax.rsqrt(jnp.sum(c * c, axis=-1, keepdims=True) + L2_EPS)
        else:
            v_ref[:, osl] = c
    ba = ba_ref[...]
    gp = gp_ref[...]
    lane = lax.broadcasted_iota(jnp.int32, ba.shape, 1)
    bg_ref[...] = jnp.where(lane < GDN_HEADS, jax.nn.sigmoid(ba), gp[0:1] * _softplus(ba + gp[1:2]))


def _gdn_pre(conv_in, conv_prev8, conv_w, ba, gparams, bm, name):
    nb, t, c3 = conv_in.shape
    w = c3 // 3
    bm = min(bm, t)
    hb = bm // 8
    row = lambda b, j: (b, j, 0)
    out = jax.ShapeDtypeStruct((nb, t, w), F32)
    return pl.pallas_call(
        _gdn_pre_kernel,
        grid=(nb, t // bm),
        in_specs=[pl.BlockSpec((None, bm, c3), row),
                  pl.BlockSpec((None, 8, c3), lambda b, j: (b, jnp.maximum(j * hb - 1, 0), 0)),
                  pl.BlockSpec((None, 8, c3), lambda b, j: (b, 0, 0)),
                  pl.BlockSpec((CONV_WIDTH, c3), lambda b, j: (0, 0)),
                  pl.BlockSpec((None, bm, LANES), row),
                  pl.BlockSpec((8, LANES), lambda b, j: (0, 0))],
        out_specs=[pl.BlockSpec((None, bm, w), row)] * 3 + [pl.BlockSpec((None, bm, LANES), row)],
        out_shape=[out] * 3 + [jax.ShapeDtypeStruct((nb, t, LANES), F32)],
        scratch_shapes=[pltpu.VMEM((bm + 8, GDN_DIM), F32)],
        compiler_params=_params("parallel", "parallel"),
        name=name,
    )(conv_in, conv_in, conv_prev8, conv_w, ba, gparams)


def _gdn_prep_kernel(q_ref, k_ref, v_ref, bg_ref, u_ref, w_ref, qd_ref, kd_ref, intra_ref, gl_ref,
                     *, chunks):
    n = GDN_CHUNK
    ri = lax.broadcasted_iota(jnp.int32, (n, n), 0)
    ci = lax.broadcasted_iota(jnp.int32, (n, n), 1)
    incl = ri >= ci
    strict = ri > ci
    lane_pad = jnp.zeros((n, GDN_DIM - n), F32)
    rowid = lax.broadcasted_iota(jnp.int32, (n, GDN_DIM), 0)
    unroll = GDN_PREP_UNROLL if chunks % GDN_PREP_UNROLL == 0 else 1

    def chunk(cc, carry):
        cs = [cc * unroll + d for d in range(unroll)]
        hs = range(GDN_HEADS * unroll)
        sls = [slice((h % GDN_HEADS) * GDN_DIM, (h % GDN_HEADS + 1) * GDN_DIM) for h in hs]
        rws = [pl.ds(pl.multiple_of(cs[h // GDN_HEADS] * n, n), n) for h in hs]
        ks = [k_ref[r, sl] for r, sl in zip(rws, sls)]
        bgs = [bg_ref[r, :] for r in rws[::GDN_HEADS]]
        lanes = lambda col: jnp.broadcast_to(col, (n, GDN_DIM))
        betas = [lanes(bgs[h // GDN_HEADS][:, h % GDN_HEADS:h % GDN_HEADS + 1]) for h in hs]
        gs = [lanes(bgs[h // GDN_HEADS][:, GDN_HEADS + h % GDN_HEADS:GDN_HEADS + h % GDN_HEADS + 1]) for h in hs]
        gcums = [_prefix_rows(g, rowid) for g in gs]
        grows = [jnp.broadcast_to(gc.T[0:1, :], (n, n)) for gc in gcums]
        kbfs = [k.astype(BF16) for k in ks]
        kbs = [k * beta for k, beta in zip(ks, betas)]
        kks = [_dot_nt(kb.astype(BF16), kbf) for kb, kbf in zip(kbs, kbfs)]
        qks = [_dot_nt(q_ref[r, sl].astype(BF16), kbf) for r, sl, kbf in zip(rws, sls, kbfs)]
        gams = [jnp.where(incl, jnp.exp(jnp.where(incl, gc[:, :n] - gr, 0.0)), 0.0) for gc, gr in zip(gcums, grows)]
        egs = [jnp.exp(gc) for gc in gcums]
        mps = [-jnp.where(strict, kk * gam, 0.0) for kk, gam in zip(kks, gams)]
        sols = [jnp.concatenate([v_ref[r, sl] * beta, kb * eg], axis=-1)
                for r, sl, beta, kb, eg in zip(rws, sls, betas, kbs, egs)]
        span = 1
        while span < n:
            mpbs = [mp.astype(BF16) for mp in mps]
            if span < (1 << GDN_SPLIT_STEPS):
                upds = [_dot(mpb, jnp.concatenate(_split2(sol), axis=-1)) for mpb, sol in zip(mpbs, sols)]
                sols = [sol + (upd[:, :2 * GDN_DIM] + upd[:, 2 * GDN_DIM:]) for sol, upd in zip(sols, upds)]
            else:
                sols = [sol + _dot(mpb, sol.astype(BF16)) for mpb, sol in zip(mpbs, sols)]
            span *= 2
            if span < n:
                mps = [_dot(mpb, mpb) for mpb in mpbs]
        for h in hs:
            sl, rows, head = sls[h], rws[h], h % GDN_HEADS
            glast = gcums[h][n - 1:n, :]
            u_ref[rows, sl] = sols[h][:, :GDN_DIM]
            w_ref[rows, sl] = sols[h][:, GDN_DIM:].astype(BF16)
            intra = jnp.where(incl, qks[h] * gams[h], 0.0)
            intra_ref[rows, sl] = jnp.concatenate([intra, lane_pad], axis=-1).astype(BF16)
            qd_ref[rows, sl] = (q_ref[rows, sl] * egs[h]).astype(BF16)
            kd_ref[rows, sl] = (ks[h] * jnp.exp(glast - gcums[h])).astype(BF16)
            gl_ref[cs[h // GDN_HEADS], head:head + 1, :] = jnp.exp(glast)
        return carry

    lax.fori_loop(0, chunks // unroll, chunk, 0)


def _gdn_scan_kernel(u_ref, w_ref, qd_ref, kd_ref, intra_ref, gl_ref, s0_ref, o_ref, sfin_ref, s_scr, *, chunks):
    n = GDN_CHUNK
    cg = pl.program_id(1)

    @pl.when(cg == 0)
    def _():
        s_scr[...] = s0_ref[...]

    def chunk(c, carry):
        rows = pl.ds(pl.multiple_of(c * n, n), n)
        hs = range(GDN_HEADS)
        sls = [slice(h * GDN_DIM, (h + 1) * GDN_DIM) for h in hs]
        ss = [s_scr[h] for h in hs]
        rs = [_dot(jnp.concatenate([w_ref[rows, sl], qd_ref[rows, sl]], axis=0), s.astype(BF16))
              for sl, s in zip(sls, ss)]
        vbs = [(u_ref[rows, sl] - r[:n]).astype(BF16) for sl, r in zip(sls, rs)]
        upds = [_dot_tn(kd_ref[rows, sl], vb) for sl, vb in zip(sls, vbs)]
        for h in hs:
            s_scr[h] = ss[h] * gl_ref[c, h:h + 1, :] + upds[h]
        for h in hs:
            o_ref[rows, sls[h]] = rs[h][n:] + _dot(intra_ref[rows, h * GDN_DIM:h * GDN_DIM + n], vbs[h])
        return carry

    lax.fori_loop(0, chunks, chunk, 0)

    @pl.when(cg == pl.num_programs(1) - 1)
    def _():
        sfin_ref[...] = s_scr[...]


def _gdn(q, k, v, bg, s0, tag):
    nb, t, w = q.shape
    nc = t // GDN_CHUNK
    cgs = min(GDN_CHUNK_GROUP, nc)
    rows = cgs * GDN_CHUNK
    tok = pl.BlockSpec((None, rows, w), lambda b, c: (b, c, 0))
    gl = pl.BlockSpec((None, cgs, GDN_HEADS, GDN_DIM), lambda b, c: (b, c, 0, 0))
    st = pl.BlockSpec((None, GDN_HEADS, GDN_DIM, GDN_DIM), lambda b, c: (b, 0, 0, 0))
    tok_f32 = jax.ShapeDtypeStruct((nb, t, w), F32)
    tok_bf16 = jax.ShapeDtypeStruct((nb, t, w), BF16)
    gl_shape = jax.ShapeDtypeStruct((nb, nc, GDN_HEADS, GDN_DIM), F32)
    u, wy, qd, kd, intra, glast = pl.pallas_call(
        functools.partial(_gdn_prep_kernel, chunks=cgs),
        grid=(nb, nc // cgs),
        in_specs=[tok] * 3 + [pl.BlockSpec((None, rows, LANES), lambda b, c: (b, c, 0))],
        out_specs=[tok] * 5 + [gl],
        out_shape=[tok_f32] + [tok_bf16] * 4 + [gl_shape],
        compiler_params=_params("parallel", "parallel"),
        name=f"gdn_prep_{tag}",
    )(q, k, v, bg)
    return pl.pallas_call(
        functools.partial(_gdn_scan_kernel, chunks=cgs),
        grid=(nb, nc // cgs),
        in_specs=[tok] * 5 + [gl, st],
        out_specs=[tok, st],
        out_shape=[tok_f32, jax.ShapeDtypeStruct((nb, GDN_HEADS, GDN_DIM, GDN_DIM), F32)],
        scratch_shapes=[pltpu.VMEM((GDN_HEADS, GDN_DIM, GDN_DIM), F32)],
        compiler_params=_params("parallel", "arbitrary"),
        name=f"gdn_scan_{tag}",
    )(u, wy, qd, kd, intra, glast, s0)


def _sbp_kernel(bias_ref, qt_ref, k_ref, vt_ref, u2_ref, o_ref, acc_ref):
    hg = pl.program_id(0)
    i = pl.program_id(1)
    nh = qt_ref.shape[0]
    u2 = u2_ref[...]
    tk, tq = SB_TK, SB_TQ

    def blocks(js, accs, offs):
        valid = [None if o is None else
                 (lax.broadcasted_iota(jnp.int32, (tk, tq), 0) + o) < lax.broadcasted_iota(jnp.int32, (tk, tq), 1)
                 for o in offs]
        chains = [(b, h) for b in range(len(js)) for h in range(nh)]
        rows = [pl.ds(pl.multiple_of(j * tk, tk), tk) for j in js]
        accs = list(accs)
        nz, lg, rinc = {}, {}, {}

        def st_qk(c):
            b, h = c
            nz[c] = _dot(k_ref[rows[b], h * SB_HEAD_DIM:(h + 1) * SB_HEAD_DIM], qt_ref[h]) + bias_ref[hg * nh + h]

        def st_lg(c):
            v = _log2_one_minus_sigmoid(nz[c])
            lg[c] = v if valid[c[0]] is None else jnp.where(valid[c[0]], v, 0.0)

        def st_sum(c):
            rinc[c] = _dot(u2, jnp.concatenate(_split2(lg.pop(c)), axis=0))

        def st_av(c):
            b, h = c
            a = jnp.exp2((rinc[c] + accs[h]) - nz.pop(c))
            if valid[b] is not None:
                a = jnp.where(valid[b], a, 0.0)
            acc_ref[h] += _dot(vt_ref[h, js[b]], a.astype(BF16))
            accs[h] = accs[h] + rinc.pop(c)[0:1, :]

        stages = [st_qk, st_lg, st_sum, st_av]
        for step in range(len(chains) + (len(stages) - 1) * SB_SKEW):
            for si, st in enumerate(stages):
                ci = step - si * SB_SKEW
                if 0 <= ci < len(chains):
                    st(chains[ci])
        return tuple(accs)

    acc_ref[...] = jnp.zeros_like(acc_ref)
    per = tq // tk
    nb = SB_KEY_BLOCKS
    first = i * per
    accs = blocks([first + per - 1 - d for d in range(per)], tuple(jnp.zeros((1, tq), F32) for _ in range(nh)),
                  [(per - 1 - d) * tk for d in range(per)])
    accs = lax.fori_loop(0, first // nb,
                         lambda n, c: blocks([first - 1 - nb * n - d for d in range(nb)], c, [None] * nb), accs)
    done = (first // nb) * nb
    size = nb // 2
    while size >= per:
        take = ((first - done) // size) % 2
        start = first - done - 1
        accs = lax.fori_loop(0, take, lambda n, c, start=start, size=size:
                             blocks([start - d for d in range(size)], c, [None] * size), accs)
        done = done + take * size
        size //= 2
    for h in range(nh):
        o_ref[:, h * SB_HEAD_DIM:(h + 1) * SB_HEAD_DIM] = acc_ref[h].T.astype(BF16)


def _sb_prompt(qt, kb, vt, bias):
    hh, dh, s = qt.shape
    hg = SB_HEAD_GROUP
    nkb = s // SB_TK
    r = jnp.arange(SB_TK)
    u = (r[None, :] >= r[:, None]).astype(BF16)
    u2 = jnp.concatenate([u, u], axis=1)
    once = pl.Buffered(1)
    return pl.pallas_call(
        _sbp_kernel,
        grid=(hh // hg, s // SB_TQ),
        in_specs=[pl.BlockSpec(memory_space=pltpu.SMEM),
                  pl.BlockSpec((hg, dh, SB_TQ), lambda g, i: (g, 0, i)),
                  pl.BlockSpec((s, hg * dh), lambda g, i: (0, g), pipeline_mode=once),
                  pl.BlockSpec((hg, nkb, dh, SB_TK), lambda g, i: (g, 0, 0, 0), pipeline_mode=once),
                  pl.BlockSpec((SB_TK, 2 * SB_TK), lambda g, i: (0, 0), pipeline_mode=once)],
        out_specs=pl.BlockSpec((SB_TQ, hg * dh), lambda g, i: (i, g)),
        out_shape=jax.ShapeDtypeStruct((s, hh * dh), BF16),
        scratch_shapes=[pltpu.VMEM((hg, dh, SB_TQ), F32)],
        compiler_params=_params("parallel", "arbitrary"),
        name="sb_prompt",
    )(bias.astype(F32) * (-LOG2E), qt, kb, vt, u2)


def _sbs_kernel(pt_ref, qbd_ref, bias_ref, knew_ref, vnew_ref, *rest, pages):
    kc_refs, vc_refs = rest[:pages], rest[pages:2 * pages]
    u2_ref, o_ref, acc_scr, out_scr = rest[2 * pages:]
    p = pl.program_id(1)
    qbd = qbd_ref[...]
    bias = bias_ref[...]
    u2 = u2_ref[...]
    nrow, npos = bias.shape
    t = nrow // SB_HEADS

    def page_rows(ref):
        if ref.shape[1] != SB_HEAD_DIM:
            return ref[...].astype(BF16)
        return jnp.concatenate([ref[pl.ds(h, npos, stride=SB_HEADS), :].astype(BF16) for h in range(SB_HEADS)],
                               axis=1)

    def group(k_refs, v_refs, masked):
        kcat = jnp.concatenate([page_rows(r) for r in k_refs], axis=0)
        vcat = jnp.concatenate([page_rows(r) for r in v_refs], axis=0)
        zall = _dot_nt(qbd, kcat)
        acc = acc_scr[...]
        probs = []
        for gi in range(len(k_refs)):
            z = zall[:, gi * npos:(gi + 1) * npos] + bias
            lg = _neg_softplus(z)
            if masked:
                tok = lax.broadcasted_iota(jnp.int32, (nrow, npos), 0) % t
                valid = lax.broadcasted_iota(jnp.int32, (nrow, npos), 1) < tok
                lg = jnp.where(valid, lg, 0.0)
            hi, lo = _split2(lg)
            rinc = _dot(jnp.concatenate([hi, lo], axis=1), u2)
            a = jnp.exp(z + rinc + acc)
            if masked:
                a = jnp.where(valid, a, 0.0)
            probs.append(a.astype(BF16))
            acc = acc + jnp.sum(lg, axis=-1, keepdims=True)
        out_scr[...] += _dot(jnp.concatenate(probs, axis=1), vcat)
        acc_scr[...] = acc

    @pl.when(p == 0)
    def _():
        acc_scr[...] = jnp.zeros_like(acc_scr)
        out_scr[...] = jnp.zeros_like(out_scr)
        group([knew_ref], [vnew_ref], True)

    group(kc_refs, vc_refs, False)

    @pl.when(p == pl.num_programs(1) - 1)
    def _():
        for h in range(SB_HEADS):
            sl = slice(h * SB_HEAD_DIM, (h + 1) * SB_HEAD_DIM)
            o_ref[:, sl] = out_scr[h * t:(h + 1) * t, sl]


def _sb_sample(q, k, v, bias, cache_k, cache_v, page_ids):
    nb, t, w = q.shape
    hh, dh = SB_HEADS, SB_HEAD_DIM
    npg = page_ids.shape[1]
    psz = cache_k.shape[1] // hh
    pgs = min(SB_PAGE_GROUP, npg)
    qs = (q * (dh ** -0.5)).astype(BF16).reshape(nb, t, hh, dh)
    eye = jnp.eye(hh, dtype=BF16)
    qbd = (qs.transpose(0, 2, 1, 3)[:, :, :, None, :] * eye[None, :, None, :, None]).reshape(nb, hh * t, w)
    bias_rep = jnp.broadcast_to(jnp.repeat(bias.astype(F32), t)[:, None], (hh * t, psz))
    pad = ((0, 0), (0, psz - t), (0, 0))
    knew, vnew = jnp.pad(k, pad), jnp.pad(v, pad)
    r = jnp.arange(psz)
    u = (r[:, None] >= r[None, :]).astype(BF16)
    u2 = jnp.concatenate([u, u], axis=0)
    per_seq = lambda b, p, pt: (b, 0, 0)
    cache = [pl.BlockSpec((None, psz * hh, dh), lambda b, p, pt, gi=gi: (pt[b, npg - 1 - (p * pgs + gi)], 0, 0))
             for gi in range(pgs)]
    return pl.pallas_call(
        functools.partial(_sbs_kernel, pages=pgs),
        grid_spec=pltpu.PrefetchScalarGridSpec(
            num_scalar_prefetch=1,
            grid=(nb, npg // pgs),
            in_specs=[pl.BlockSpec((None, hh * t, w), per_seq),
                      pl.BlockSpec((hh * t, psz), lambda b, p, pt: (0, 0)),
                      pl.BlockSpec((None, psz, w), per_seq),
                      pl.BlockSpec((None, psz, w), per_seq)]
                     + cache + cache
                     + [pl.BlockSpec((2 * psz, psz), lambda b, p, pt: (0, 0))],
            out_specs=pl.BlockSpec((None, t, w), per_seq),
            scratch_shapes=[pltpu.VMEM((hh * t, psz), F32), pltpu.VMEM((hh * t, w), F32)],
        ),
        out_shape=jax.ShapeDtypeStruct((nb, t, w), F32),
        compiler_params=_params("parallel", "arbitrary"),
        name="sb_sample",
    )(page_ids, qbd, bias_rep, knew, vnew, *([cache_k] * pgs), *([cache_v] * pgs), u2)


def _merge_kernel(oa_ref, ob_ref, zg_ref, ga_ref, gb_ref, x_ref, nw_ref, woa_ref, wob_ref, wout_ref,
                  g1_ref, b1_ref, h_ref, hb_ref, *, alpha):
    ob = ob_ref[...]
    zg = zg_ref[...]
    nw = nw_ref[...]
    parts = []
    for h in range(GDN_HEADS):
        sl = slice(h * GDN_DIM, (h + 1) * GDN_DIM)
        oh = ob[:, sl]
        zh = zg[:, sl]
        rn = oh * lax.rsqrt(jnp.mean(oh * oh, axis=-1, keepdims=True) + RMS_EPS)
        parts.append((rn * nw * (zh * jax.nn.sigmoid(zh))).astype(BF16))
    obn = jnp.concatenate(parts, axis=-1)
    merged = (jax.nn.sigmoid(ga_ref[...]) * _dot(oa_ref[...], woa_ref[...])
              + jax.nn.sigmoid(gb_ref[...]) * _dot(obn, wob_ref[...]))
    y = alpha * x_ref[...] + _dot(merged.astype(BF16), wout_ref[...])
    hn = _layernorm(y, g1_ref[...], b1_ref[...])
    h_ref[...] = hn
    hb_ref[...] = hn.astype(BF16)


def _merge(oa, ob, zg, gab, x, nw, woa, wob, wout, g1, b1, alpha, bm, name):
    m, d = x.shape
    w = oa.shape[1]
    bm = min(bm, m)
    row = lambda i: (i, 0)
    fix = lambda i: (0, 0)
    once = functools.partial(pl.BlockSpec, index_map=fix, pipeline_mode=pl.Buffered(1))
    return pl.pallas_call(
        functools.partial(_merge_kernel, alpha=alpha),
        grid=(m // bm,),
        in_specs=[pl.BlockSpec((bm, w), row), pl.BlockSpec((bm, w), row), pl.BlockSpec((bm, w), row),
                  pl.BlockSpec((bm, d), row), pl.BlockSpec((bm, d), lambda i: (i, 1)), pl.BlockSpec((bm, d), row),
                  pl.BlockSpec((1, GDN_DIM), fix),
                  once((w, d)), once((w, d)), once((d, d)),
                  pl.BlockSpec((1, d), fix), pl.BlockSpec((1, d), fix)],
        out_specs=[pl.BlockSpec((bm, d), row), pl.BlockSpec((bm, d), row)],
        out_shape=[jax.ShapeDtypeStruct((m, d), F32), jax.ShapeDtypeStruct((m, d), BF16)],
        compiler_params=_params("parallel"),
        name=name,
    )(oa, ob, zg, gab, gab, x, nw, woa, wob, wout, g1, b1)


def _ffn_kernel(hb_ref, h_ref, wg_ref, wu_ref, wd_ref, g2_ref, b2_ref, y_ref, acc_ref, *, alpha):
    f = pl.program_id(1)
    hb = hb_ref[...]
    gate = _dot(hb, wg_ref[...])
    up = _dot(hb, wu_ref[...])
    act = (gate * jax.nn.sigmoid(gate) * up).astype(BF16)
    part = _dot(act, wd_ref[...])

    @pl.when(f == 0)
    def _():
        acc_ref[...] = part

    @pl.when(f != 0)
    def _():
        acc_ref[...] += part

    @pl.when(f == pl.num_programs(1) - 1)
    def _():
        y_ref[...] = _layernorm(alpha * h_ref[...] + acc_ref[...], g2_ref[...], b2_ref[...])


def _ffn(hb, h, wgu, wd, g2, b2, alpha, bm, bf, name):
    m, d = h.shape
    dff = wd.shape[0]
    nf = dff // bf
    bm = min(bm, m)
    row = lambda i, f: (i, 0)
    fix = lambda i, f: (0, 0)
    return pl.pallas_call(
        functools.partial(_ffn_kernel, alpha=alpha),
        grid=(m // bm, nf),
        in_specs=[pl.BlockSpec((bm, d), row), pl.BlockSpec((bm, d), row),
                  pl.BlockSpec((d, bf), lambda i, f: (0, f)), pl.BlockSpec((d, bf), lambda i, f: (0, nf + f)),
                  pl.BlockSpec((bf, d), lambda i, f: (f, 0)),
                  pl.BlockSpec((1, d), fix), pl.BlockSpec((1, d), fix)],
        out_specs=pl.BlockSpec((bm, d), row),
        out_shape=jax.ShapeDtypeStruct((m, d), F32),
        scratch_shapes=[pltpu.VMEM((bm, d), F32)],
        compiler_params=_params("parallel", "arbitrary"),
        name=name,
    )(hb, h, wgu, wgu, wd, g2, b2)


def _layer(x, sb_mix, conv_prev, s_prev, wts, alpha, tag, bm):
    nb, t, d = x.shape
    m = nb * t
    w = SB_HEADS * SB_HEAD_DIM
    xf = x.reshape(m, d)
    cols = wts["in_cols"]

    def project(xb, nm):
        col0, n, bn = cols[nm]
        return _matmul(xb, wts["w_main"], bm, bn, f"proj_{nm}_{tag}", col0, n)

    xb, o_a, ka, va = sb_mix(xf, project)
    conv_in = project(xb, "conv").reshape(nb, t, 3 * w)
    zg, ba = project(xb, "z"), project(xb, "ba")
    gab = _matmul(xb, wts["w_gab"], bm, 1024, f"proj_gab_{tag}")

    prev8 = jnp.pad(conv_prev.astype(F32), ((0, 0), (8 - (CONV_WIDTH - 1), 0), (0, 0)))
    qn, kn, vv, bg = _gdn_pre(conv_in, prev8, wts["conv_w"], ba.reshape(nb, t, LANES), wts["gparams"],
                                   256, f"gdn_pre_{tag}")
    tp = -(-t // GDN_CHUNK) * GDN_CHUNK
    if tp != t:
        padt = lambda a: jnp.pad(a, ((0, 0), (0, tp - t), (0, 0)))
        qn, kn, vv, bg = map(padt, (qn, kn, vv, bg))
    o_b, s_new = _gdn(qn, kn, vv, bg, s_prev.astype(F32), tag)
    o_b = o_b[:, :t].reshape(m, w)

    h, hb = _merge(o_a, o_b, zg, gab, xf, wts["norm_w"], wts["w_o_a"], wts["w_o_b"], wts["w_out"],
                   wts["ln1_g"], wts["ln1_b"], alpha, 256, f"merge_{tag}")
    y = _ffn(hb, h, wts["w_gu"], wts["w_down"], wts["ln2_g"], wts["ln2_b"], alpha, 512, 512, f"ffn_{tag}")
    conv_new = jnp.concatenate([conv_prev.astype(F32), conv_in], axis=1)[:, -(CONV_WIDTH - 1):]
    return (y.reshape(nb, t, d),
            (ka.reshape(nb, t, SB_HEADS, SB_HEAD_DIM), va.reshape(nb, t, SB_HEADS, SB_HEAD_DIM), s_new, conv_new))


def _prep_weights(w_in, sb_bias, conv_w, a_log, dt_bias, gdn_norm_w, w_o_a, w_o_b, w_out,
                  ln1_g, ln1_b, w_gu, w_down, ln2_g, ln2_b):
    w = SB_HEADS * SB_HEAD_DIM
    d = w_in.shape[0]
    hh = GDN_HEADS
    o = 0
    in_cols = {}
    for nm, width, bn in (("q", w, w), ("k", w, w), ("v", w, w), ("conv", 3 * w, w), ("z", w, w),
                          ("ba", 2 * hh, LANES)):
        in_cols[nm] = (o, max(width, bn), bn)
        o += width
    o_gab = o
    assert in_cols["ba"][0] + LANES <= w_in.shape[1] and o_gab + 2 * d == w_in.shape[1]
    gp = jnp.zeros((8, LANES), F32)
    gp = gp.at[0, hh:2 * hh].set(-jnp.exp(a_log.astype(F32))).at[1, hh:2 * hh].set(dt_bias.astype(F32))
    w_main = w_in.astype(BF16)
    return dict(
        in_cols=in_cols, w_main=w_main, w_gab=w_main[:, o_gab:], w_q_t=w_main[:, :w].T,
        sb_bias=sb_bias, conv_w=conv_w.astype(F32), gparams=gp,
        norm_w=gdn_norm_w.astype(F32).reshape(1, GDN_DIM),
        w_o_a=w_o_a.astype(BF16), w_o_b=w_o_b.astype(BF16), w_out=w_out.astype(BF16),
        ln1_g=ln1_g.reshape(1, d), ln1_b=ln1_b.reshape(1, d),
        w_gu=w_gu.astype(BF16), w_down=w_down.astype(BF16),
        ln2_g=ln2_g.reshape(1, d), ln2_b=ln2_b.reshape(1, d))


def kernel(x_prompt, x_sample, cache_k, cache_v, state_gdn, state_conv, page_table, w_in, sb_bias, conv_w,
           a_log, dt_bias, gdn_norm_w, w_o_a, w_o_b, w_out, ln1_g, ln1_b, w_gu, w_down, ln2_g, ln2_b):
    depth = w_in.shape[0]
    alpha = (2.0 * depth) ** 0.25
    w = SB_HEADS * SB_HEAD_DIM
    nphys, psz = cache_k.shape[1], cache_k.shape[2]
    ck = cache_k.reshape(depth * nphys, psz * SB_HEADS, SB_HEAD_DIM)
    cv = cache_v.reshape(depth * nphys, psz * SB_HEADS, SB_HEAD_DIM)
    yp, ys = x_prompt, x_sample
    outs = [[] for _ in range(8)]
    for l in range(depth):
        wts = _prep_weights(w_in[l], sb_bias[l], conv_w[l], a_log[l], dt_bias[l], gdn_norm_w[l], w_o_a[l],
                            w_o_b[l], w_out[l], ln1_g[l], ln1_b[l], w_gu[l], w_down[l], ln2_g[l], ln2_b[l])
        nbp, tp = yp.shape[0], yp.shape[1]
        conv0 = jnp.zeros((nbp, CONV_WIDTH - 1, 3 * w), F32)
        s0 = jnp.zeros((nbp, GDN_HEADS, GDN_DIM, GDN_DIM), F32)

        assert nbp == 1, "prompt attention treats all prompt rows as one sequence"

        def sb_p(xf, project):
            k_col, v_col = (wts["in_cols"][nm][0] // w for nm in ("k", "v"))
            xb, qt, kf, kb, vf, vt = _attention_projections(xf, wts["w_q_t"], wts["w_main"], k_col, v_col, 1024, "p")
            return xb, _sb_prompt(qt.reshape(SB_HEADS, SB_HEAD_DIM, tp), kb, vt, wts["sb_bias"]), kf, vf

        yp, (kp, vp, sp, cp) = _layer(yp, sb_p, conv0, s0, wts, alpha, "p", 1024)

        nbs, ts = ys.shape[0], ys.shape[1]
        page_ids = page_table + l * nphys

        def sb_s(xf, project):
            xb = xf.astype(BF16)
            q, k, v = project(xb, "q"), project(xb, "k"), project(xb, "v")
            r3 = lambda a: a.reshape(nbs, ts, w)
            o = _sb_sample(r3(q), r3(k), r3(v), wts["sb_bias"], ck, cv, page_ids)
            return xb, o.reshape(nbs * ts, w).astype(BF16), k, v

        ys, (ksm, vsm, ssm, csm) = _layer(ys, sb_s, state_conv[l], state_gdn[l], wts, alpha, "s", 256)
        for lst, val in zip(outs, (kp, vp, sp, cp, ksm, vsm, ssm, csm)):
            lst.append(val)
    return (yp, ys) + tuple(jnp.stack(o) for o in outs)
```

```python
import functools

import jax
import jax.numpy as jnp
from jax import lax
from jax.experimental import pallas as pl
from jax.experimental.pallas import tpu as pltpu

F32 = jnp.float32
BF16 = jnp.bfloat16

LANES = 128
VMEM_LIMIT = 56 * 1024 * 1024

SB_HEADS = 8
SB_HEAD_DIM = 128
GDN_HEADS = 8
GDN_DIM = 128
CONV_WIDTH = 4
GDN_CHUNK = 64
LN_EPS = 1e-5
RMS_EPS = 1e-6
L2_EPS = 1e-6

SB_TQ = 256
SB_TK = 128
SB_HEAD_GROUP = 4
SB_KEY_BLOCKS = 16
SB_SKEW = 2
SB_PAGE_GROUP = 16
GDN_SPLIT_STEPS = 3
GDN_CHUNK_GROUP = 8
GDN_PREP_UNROLL = 4


def _params(*sem):
    return pltpu.CompilerParams(dimension_semantics=sem, vmem_limit_bytes=VMEM_LIMIT)


def _dot(a, b):
    return jnp.dot(a, b, preferred_element_type=F32)


def _dot_nt(a, b):
    return lax.dot_general(a, b, (((1,), (1,)), ((), ())), preferred_element_type=F32)


def _dot_tn(a, b):
    return lax.dot_general(a, b, (((0,), (0,)), ((), ())), preferred_element_type=F32)


def _split2(x):
    hi = x.astype(BF16)
    lo = (x - hi.astype(F32)).astype(BF16)
    return hi, lo


def _prefix_rows(x, rowid):
    d = 1
    while d < x.shape[0]:
        x = x + jnp.where(rowid >= d, pltpu.roll(x, d, 0), 0.0)
        d *= 2
    return x


def _neg_softplus(z):
    return jnp.minimum(-z, 0.0) - jnp.log(1.0 + jnp.exp(-jnp.abs(z)))


LOG2E = 1.4426950408889634


def _log2_one_minus_sigmoid(nz):
    nabs = pltpu.bitcast(pltpu.bitcast(nz, jnp.uint32) | jnp.uint32(0x80000000), F32)
    return jnp.minimum(nz, 0.0) - jnp.log(1.0 + jnp.exp2(nabs)) * LOG2E


def _softplus(z):
    return jnp.maximum(z, 0.0) + jnp.log1p(jnp.exp(-jnp.abs(z)))


def _layernorm(x, g, b):
    mu = jnp.mean(x, axis=-1, keepdims=True)
    xc = x - mu
    var = jnp.mean(xc * xc, axis=-1, keepdims=True)
    return xc * lax.rsqrt(var + LN_EPS) * g + b


def _mm_kernel(x_ref, w_ref, o_ref):
    o_ref[...] = _dot(x_ref[...], w_ref[...])


def _matmul(x, w, bm, bn, name, col0=0, n=None):
    m, k = x.shape
    n = w.shape[1] if n is None else n
    bm, bn = min(bm, m), min(bn, n)
    assert col0 % bn == 0 and n % bn == 0 and col0 + n <= w.shape[1]
    first = col0 // bn
    return pl.pallas_call(
        _mm_kernel,
        grid=(n // bn, m // bm),
        in_specs=[pl.BlockSpec((bm, k), lambda j, i: (i, 0)),
                  pl.BlockSpec((k, bn), lambda j, i: (0, first + j))],
        out_specs=pl.BlockSpec((bm, bn), lambda j, i: (i, j)),
        out_shape=jax.ShapeDtypeStruct((m, n), F32),
        compiler_params=_params("parallel", "parallel"),
        name=name,
    )(x, w)


def _mm_dual_kernel(x_ref, w_ref, o_ref, ob_ref):
    r = _dot(x_ref[...], w_ref[...])
    o_ref[...] = r
    ob_ref[...] = r.astype(BF16)


def _mm_nt_kernel(wt_ref, x_ref, o_ref, xb_ref, *, scale):
    xb = x_ref[...].astype(BF16)
    xb_ref[...] = xb
    o_ref[...] = (_dot_nt(wt_ref[...], xb) * scale).astype(BF16)


def _mm_vt_kernel(x_ref, w_ref, o_ref, vt_ref):
    r = _dot(x_ref[...], w_ref[...])
    o_ref[...] = r
    for h in range(vt_ref.shape[0]):
        for kb in range(vt_ref.shape[1]):
            tile = r[kb * SB_TK:(kb + 1) * SB_TK, h * SB_HEAD_DIM:(h + 1) * SB_HEAD_DIM]
            vt_ref[h, kb] = tile.T.astype(BF16)


def _attention_projections(x, wq_t, w, k_col, v_col, bm, tag):
    m, k = x.shape
    n = wq_t.shape[0]
    rows = pl.BlockSpec((bm, k), lambda i: (i, 0))
    out = pl.BlockSpec((bm, n), lambda i: (i, 0))
    par = _params("parallel")
    qt, xb = pl.pallas_call(
        functools.partial(_mm_nt_kernel, scale=-(SB_HEAD_DIM ** -0.5) * LOG2E),
        grid=(m // bm,),
        in_specs=[pl.BlockSpec((n, k), lambda i: (0, 0)), rows],
        out_specs=[pl.BlockSpec((n, bm), lambda i: (0, i)), rows],
        out_shape=[jax.ShapeDtypeStruct((n, m), BF16), jax.ShapeDtypeStruct((m, k), BF16)],
        compiler_params=par, name=f"proj_qt_{tag}",
    )(wq_t, x)
    kf, kb = pl.pallas_call(
        _mm_dual_kernel,
        grid=(m // bm,),
        in_specs=[rows, pl.BlockSpec((k, n), lambda i: (0, k_col))],
        out_specs=[out, out],
        out_shape=[jax.ShapeDtypeStruct((m, n), F32), jax.ShapeDtypeStruct((m, n), BF16)],
        compiler_params=par, name=f"proj_k_{tag}",
    )(xb, w)
    heads = n // SB_HEAD_DIM
    vf, vt = pl.pallas_call(
        _mm_vt_kernel,
        grid=(m // bm,),
        in_specs=[rows, pl.BlockSpec((k, n), lambda i: (0, v_col))],
        out_specs=[out, pl.BlockSpec((heads, bm // SB_TK, SB_HEAD_DIM, SB_TK), lambda i: (0, i, 0, 0))],
        out_shape=[jax.ShapeDtypeStruct((m, n), F32),
                   jax.ShapeDtypeStruct((heads, m // SB_TK, SB_HEAD_DIM, SB_TK), BF16)],
        compiler_params=par, name=f"proj_v_{tag}",
    )(xb, w)
    return xb, qt, kf, kb, vf, vt


def _gdn_pre_kernel(cur_ref, halo_ref, prev_ref, cw_ref, ba_ref, gp_ref, q_ref, k_ref, v_ref, bg_ref, ext_scr):
    j = pl.program_id(1)
    bm = cur_ref.shape[0]
    w = GDN_HEADS * GDN_DIM
    for blk in range(3 * GDN_HEADS):
        sl = slice(blk * GDN_DIM, (blk + 1) * GDN_DIM)
        cur = cur_ref[:, sl]
        ext_scr[0:8, :] = jnp.where(j == 0, prev_ref[:, sl], halo_ref[:, sl])
        ext_scr[8:, :] = cur
        conv = cur * cw_ref[CONV_WIDTH - 1:CONV_WIDTH, sl]
        for d in range(1, CONV_WIDTH):
            conv = conv + ext_scr[pl.ds(8 - d, bm), :] * cw_ref[CONV_WIDTH - 1 - d:CONV_WIDTH - d, sl]
        c = conv * jax.nn.sigmoid(conv)
        osl = slice(sl.start % w, sl.start % w + GDN_DIM)
        if blk < GDN_HEADS:
            q_ref[:, osl] = c * lax.rsqrt(jnp.sum(c * c, axis=-1, keepdims=True) + L2_EPS) * (GDN_DIM ** -0.5)
        elif blk < 2 * GDN_HEADS:
            k_ref[:, osl] = c * lax.rsqrt(jnp.sum(c * c, axis=-1, keepdims=True) + L2_EPS)
        else:
            v_ref[:, osl] = c
    ba = ba_ref[...]
    gp = gp_ref[...]
    lane = lax.broadcasted_iota(jnp.int32, ba.shape, 1)
    bg_ref[...] = jnp.where(lane < GDN_HEADS, jax.nn.sigmoid(ba), gp[0:1] * _softplus(ba + gp[1:2]))


def _gdn_pre(conv_in, conv_prev8, conv_w, ba, gparams, bm, name):
    nb, t, c3 = conv_in.shape
    w = c3 // 3
    bm = min(bm, t)
    hb = bm // 8
    row = lambda b, j: (b, j, 0)
    out = jax.ShapeDtypeStruct((nb, t, w), F32)
    return pl.pallas_call(
        _gdn_pre_kernel,
        grid=(nb, t // bm),
        in_specs=[pl.BlockSpec((None, bm, c3), row),
                  pl.BlockSpec((None, 8, c3), lambda b, j: (b, jnp.maximum(j * hb - 1, 0), 0)),
                  pl.BlockSpec((None, 8, c3), lambda b, j: (b, 0, 0)),
                  pl.BlockSpec((CONV_WIDTH, c3), lambda b, j: (0, 0)),
                  pl.BlockSpec((None, bm, LANES), row),
                  pl.BlockSpec((8, LANES), lambda b, j: (0, 0))],
        out_specs=[pl.BlockSpec((None, bm, w), row)] * 3 + [pl.BlockSpec((None, bm, LANES), row)],
        out_shape=[out] * 3 + [jax.ShapeDtypeStruct((nb, t, LANES), F32)],
        scratch_shapes=[pltpu.VMEM((bm + 8, GDN_DIM), F32)],
        compiler_params=_params("parallel", "parallel"),
        name=name,
    )(conv_in, conv_in, conv_prev8, conv_w, ba, gparams)


def _gdn_prep_kernel(q_ref, k_ref, v_ref, bg_ref, u_ref, w_ref, qd_ref, kd_ref, intra_ref, gl_ref,
                     *, chunks):
    n = GDN_CHUNK
    ri = lax.broadcasted_iota(jnp.int32, (n, n), 0)
    ci = lax.broadcasted_iota(jnp.int32, (n, n), 1)
    incl = ri >= ci
    strict = ri > ci
    lane_pad = jnp.zeros((n, GDN_DIM - n), F32)
    rowid = lax.broadcasted_iota(jnp.int32, (n, GDN_DIM), 0)
    unroll = GDN_PREP_UNROLL if chunks % GDN_PREP_UNROLL == 0 else 1

    def chunk(cc, carry):
        cs = [cc * unroll + d for d in range(unroll)]
        hs = range(GDN_HEADS * unroll)
        sls = [slice((h % GDN_HEADS) * GDN_DIM, (h % GDN_HEADS + 1) * GDN_DIM) for h in hs]
        rws = [pl.ds(pl.multiple_of(cs[h // GDN_HEADS] * n, n), n) for h in hs]
        ks = [k_ref[r, sl] for r, sl in zip(rws, sls)]
        bgs = [bg_ref[r, :] for r in rws[::GDN_HEADS]]
        lanes = lambda col: jnp.broadcast_to(col, (n, GDN_DIM))
        betas = [lanes(bgs[h // GDN_HEADS][:, h % GDN_HEADS:h % GDN_HEADS + 1]) for h in hs]
        gs = [lanes(bgs[h // GDN_HEADS][:, GDN_HEADS + h % GDN_HEADS:GDN_HEADS + h % GDN_HEADS + 1]) for h in hs]
        gcums = [_prefix_rows(g, rowid) for g in gs]
        grows = [jnp.broadcast_to(gc.T[0:1, :], (n, n)) for gc in gcums]
        kbfs = [k.astype(BF16) for k in ks]
        kbs = [k * beta for k, beta in zip(ks, betas)]
        kks = [_dot_nt(kb.astype(BF16), kbf) for kb, kbf in zip(kbs, kbfs)]
        qks = [_dot_nt(q_ref[r, sl].astype(BF16), kbf) for r, sl, kbf in zip(rws, sls, kbfs)]
        gams = [jnp.where(incl, jnp.exp(jnp.where(incl, gc[:, :n] - gr, 0.0)), 0.0) for gc, gr in zip(gcums, grows)]
        egs = [jnp.exp(gc) for gc in gcums]
        mps = [-jnp.where(strict, kk * gam, 0.0) for kk, gam in zip(kks, gams)]
        sols = [jnp.concatenate([v_ref[r, sl] * beta, kb * eg], axis=-1)
                for r, sl, beta, kb, eg in zip(rws, sls, betas, kbs, egs)]
        span = 1
        while span < n:
            mpbs = [mp.astype(BF16) for mp in mps]
            if span < (1 << GDN_SPLIT_STEPS):
                upds = [_dot(mpb, jnp.concatenate(_split2(sol), axis=-1)) for mpb, sol in zip(mpbs, sols)]
                sols = [sol + (upd[:, :2 * GDN_DIM] + upd[:, 2 * GDN_DIM:]) for sol, upd in zip(sols, upds)]
            else:
                sols = [sol + _dot(mpb, sol.astype(BF16)) for mpb, sol in zip(mpbs, sols)]
            span *= 2
            if span < n:
                mps = [_dot(mpb, mpb) for mpb in mpbs]
        for h in hs:
            sl, rows, head = sls[h], rws[h], h % GDN_HEADS
            glast = gcums[h][n - 1:n, :]
            u_ref[rows, sl] = sols[h][:, :GDN_DIM]
            w_ref[rows, sl] = sols[h][:, GDN_DIM:].astype(BF16)
            intra = jnp.where(incl, qks[h] * gams[h], 0.0)
            intra_ref[rows, sl] = jnp.concatenate([intra, lane_pad], axis=-1).astype(BF16)
            qd_ref[rows, sl] = (q_ref[rows, sl] * egs[h]).astype(BF16)
            kd_ref[rows, sl] = (ks[h] * jnp.exp(glast - gcums[h])).astype(BF16)
            gl_ref[cs[h // GDN_HEADS], head:head + 1, :] = jnp.exp(glast)
        return carry

    lax.fori_loop(0, chunks // unroll, chunk, 0)


def _gdn_scan_kernel(u_ref, w_ref, qd_ref, kd_ref, intra_ref, gl_ref, s0_ref, o_ref, sfin_ref, s_scr, *, chunks):
    n = GDN_CHUNK
    cg = pl.program_id(1)

    @pl.when(cg == 0)
    def _():
        s_scr[...] = s0_ref[...]

    def chunk(c, carry):
        rows = pl.ds(pl.multiple_of(c * n, n), n)
        hs = range(GDN_HEADS)
        sls = [slice(h * GDN_DIM, (h + 1) * GDN_DIM) for h in hs]
        ss = [s_scr[h] for h in hs]
        rs = [_dot(jnp.concatenate([w_ref[rows, sl], qd_ref[rows, sl]], axis=0), s.astype(BF16))
              for sl, s in zip(sls, ss)]
        vbs = [(u_ref[rows, sl] - r[:n]).astype(BF16) for sl, r in zip(sls, rs)]
        upds = [_dot_tn(kd_ref[rows, sl], vb) for sl, vb in zip(sls, vbs)]
        for h in hs:
            s_scr[h] = ss[h] * gl_ref[c, h:h + 1, :] + upds[h]
        for h in hs:
            o_ref[rows, sls[h]] = rs[h][n:] + _dot(intra_ref[rows, h * GDN_DIM:h * GDN_DIM + n], vbs[h])
        return carry

    lax.fori_loop(0, chunks, chunk, 0)

    @pl.when(cg == pl.num_programs(1) - 1)
    def _():
        sfin_ref[...] = s_scr[...]


def _gdn(q, k, v, bg, s0, tag):
    nb, t, w = q.shape
    nc = t // GDN_CHUNK
    cgs = min(GDN_CHUNK_GROUP, nc)
    rows = cgs * GDN_CHUNK
    tok = pl.BlockSpec((None, rows, w), lambda b, c: (b, c, 0))
    gl = pl.BlockSpec((None, cgs, GDN_HEADS, GDN_DIM), lambda b, c: (b, c, 0, 0))
    st = pl.BlockSpec((None, GDN_HEADS, GDN_DIM, GDN_DIM), lambda b, c: (b, 0, 0, 0))
    tok_f32 = jax.ShapeDtypeStruct((nb, t, w), F32)
    tok_bf16 = jax.ShapeDtypeStruct((nb, t, w), BF16)
    gl_shape = jax.ShapeDtypeStruct((nb, nc, GDN_HEADS, GDN_DIM), F32)
    u, wy, qd, kd, intra, glast = pl.pallas_call(
        functools.partial(_gdn_prep_kernel, chunks=cgs),
        grid=(nb, nc // cgs),
        in_specs=[tok] * 3 + [pl.BlockSpec((None, rows, LANES), lambda b, c: (b, c, 0))],
        out_specs=[tok] * 5 + [gl],
        out_shape=[tok_f32] + [tok_bf16] * 4 + [gl_shape],
        compiler_params=_params("parallel", "parallel"),
        name=f"gdn_prep_{tag}",
    )(q, k, v, bg)
    return pl.pallas_call(
        functools.partial(_gdn_scan_kernel, chunks=cgs),
        grid=(nb, nc // cgs),
        in_specs=[tok] * 5 + [gl, st],
        out_specs=[tok, st],
        out_shape=[tok_f32, jax.ShapeDtypeStruct((nb, GDN_HEADS, GDN_DIM, GDN_DIM), F32)],
        scratch_shapes=[pltpu.VMEM((GDN_HEADS, GDN_DIM, GDN_DIM), F32)],
        compiler_params=_params("parallel", "arbitrary"),
        name=f"gdn_scan_{tag}",
    )(u, wy, qd, kd, intra, glast, s0)


def _sbp_kernel(bias_ref, qt_ref, k_ref, vt_ref, u2_ref, o_ref, acc_ref):
    hg = pl.program_id(0)
    i = pl.program_id(1)
    nh = qt_ref.shape[0]
    u2 = u2_ref[...]
    tk, tq = SB_TK, SB_TQ

    def blocks(js, accs, offs):
        valid = [None if o is None else
                 (lax.broadcasted_iota(jnp.int32, (tk, tq), 0) + o) < lax.broadcasted_iota(jnp.int32, (tk, tq), 1)
                 for o in offs]
        chains = [(b, h) for b in range(len(js)) for h in range(nh)]
        rows = [pl.ds(pl.multiple_of(j * tk, tk), tk) for j in js]
        accs = list(accs)
        nz, lg, rinc = {}, {}, {}

        def st_qk(c):
            b, h = c
            nz[c] = _dot(k_ref[rows[b], h * SB_HEAD_DIM:(h + 1) * SB_HEAD_DIM], qt_ref[h]) + bias_ref[hg * nh + h]

        def st_lg(c):
            v = _log2_one_minus_sigmoid(nz[c])
            lg[c] = v if valid[c[0]] is None else jnp.where(valid[c[0]], v, 0.0)

        def st_sum(c):
            rinc[c] = _dot(u2, jnp.concatenate(_split2(lg.pop(c)), axis=0))

        def st_av(c):
            b, h = c
            a = jnp.exp2((rinc[c] + accs[h]) - nz.pop(c))
            if valid[b] is not None:
                a = jnp.where(valid[b], a, 0.0)
            acc_ref[h] += _dot(vt_ref[h, js[b]], a.astype(BF16))
            accs[h] = accs[h] + rinc.pop(c)[0:1, :]

        stages = [st_qk, st_lg, st_sum, st_av]
        for step in range(len(chains) + (len(stages) - 1) * SB_SKEW):
            for si, st in enumerate(stages):
                ci = step - si * SB_SKEW
                if 0 <= ci < len(chains):
                    st(chains[ci])
        return tuple(accs)

    acc_ref[...] = jnp.zeros_like(acc_ref)
    per = tq // tk
    nb = SB_KEY_BLOCKS
    first = i * per
    accs = blocks([first + per - 1 - d for d in range(per)], tuple(jnp.zeros((1, tq), F32) for _ in range(nh)),
                  [(per - 1 - d) * tk for d in range(per)])
    accs = lax.fori_loop(0, first // nb,
                         lambda n, c: blocks([first - 1 - nb * n - d for d in range(nb)], c, [None] * nb), accs)
    done = (first // nb) * nb
    size = nb // 2
    while size >= per:
        take = ((first - done) // size) % 2
        start = first - done - 1
        accs = lax.fori_loop(0, take, lambda n, c, start=start, size=size:
                             blocks([start - d for d in range(size)], c, [None] * size), accs)
        done = done + take * size
        size //= 2
    for h in range(nh):
        o_ref[:, h * SB_HEAD_DIM:(h + 1) * SB_HEAD_DIM] = acc_ref[h].T.astype(BF16)


def _sb_prompt(qt, kb, vt, bias):
    hh, dh, s = qt.shape
    hg = SB_HEAD_GROUP
    nkb = s // SB_TK
    r = jnp.arange(SB_TK)
    u = (r[None, :] >= r[:, None]).astype(BF16)
    u2 = jnp.concatenate([u, u], axis=1)
    once = pl.Buffered(1)
    return pl.pallas_call(
        _sbp_kernel,
        grid=(hh // hg, s // SB_TQ),
        in_specs=[pl.BlockSpec(memory_space=pltpu.SMEM),
                  pl.BlockSpec((hg, dh, SB_TQ), lambda g, i: (g, 0, i)),
                  pl.BlockSpec((s, hg * dh), lambda g, i: (0, g), pipeline_mode=once),
                  pl.BlockSpec((hg, nkb, dh, SB_TK), lambda g, i: (g, 0, 0, 0), pipeline_mode=once),
                  pl.BlockSpec((SB_TK, 2 * SB_TK), lambda g, i: (0, 0), pipeline_mode=once)],
        out_specs=pl.BlockSpec((SB_TQ, hg * dh), lambda g, i: (i, g)),
        out_shape=jax.ShapeDtypeStruct((s, hh * dh), BF16),
        scratch_shapes=[pltpu.VMEM((hg, dh, SB_TQ), F32)],
        compiler_params=_params("parallel", "arbitrary"),
        name="sb_prompt",
    )(bias.astype(F32) * (-LOG2E), qt, kb, vt, u2)


def _sbs_kernel(pt_ref, qbd_ref, bias_ref, knew_ref, vnew_ref, *rest, pages):
    kc_refs, vc_refs = rest[:pages], rest[pages:2 * pages]
    u2_ref, o_ref, acc_scr, out_scr = rest[2 * pages:]
    p = pl.program_id(1)
    qbd = qbd_ref[...]
    bias = bias_ref[...]
    u2 = u2_ref[...]
    nrow, npos = bias.shape
    t = nrow // SB_HEADS

    def page_rows(ref):
        if ref.shape[1] != SB_HEAD_DIM:
            return ref[...].astype(BF16)
        return jnp.concatenate([ref[pl.ds(h, npos, stride=SB_HEADS), :].astype(BF16) for h in range(SB_HEADS)],
                               axis=1)

    def group(k_refs, v_refs, masked):
        kcat = jnp.concatenate([page_rows(r) for r in k_refs], axis=0)
        vcat = jnp.concatenate([page_rows(r) for r in v_refs], axis=0)
        zall = _dot_nt(qbd, kcat)
        acc = acc_scr[...]
        probs = []
        for gi in range(len(k_refs)):
            z = zall[:, gi * npos:(gi + 1) * npos] + bias
            lg = _neg_softplus(z)
            if masked:
                tok = lax.broadcasted_iota(jnp.int32, (nrow, npos), 0) % t
                valid = lax.broadcasted_iota(jnp.int32, (nrow, npos), 1) < tok
                lg = jnp.where(valid, lg, 0.0)
            hi, lo = _split2(lg)
            rinc = _dot(jnp.concatenate([hi, lo], axis=1), u2)
            a = jnp.exp(z + rinc + acc)
            if masked:
                a = jnp.where(valid, a, 0.0)
            probs.append(a.astype(BF16))
            acc = acc + jnp.sum(lg, axis=-1, keepdims=True)
        out_scr[...] += _dot(jnp.concatenate(probs, axis=1), vcat)
        acc_scr[...] = acc

    @pl.when(p == 0)
    def _():
        acc_scr[...] = jnp.zeros_like(acc_scr)
        out_scr[...] = jnp.zeros_like(out_scr)
        group([knew_ref], [vnew_ref], True)

    group(kc_refs, vc_refs, False)

    @pl.when(p == pl.num_programs(1) - 1)
    def _():
        for h in range(SB_HEADS):
            sl = slice(h * SB_HEAD_DIM, (h + 1) * SB_HEAD_DIM)
            o_ref[:, sl] = out_scr[h * t:(h + 1) * t, sl]


def _sb_sample(q, k, v, bias, cache_k, cache_v, page_ids):
    nb, t, w = q.shape
    hh, dh = SB_HEADS, SB_HEAD_DIM
    npg = page_ids.shape[1]
    psz = cache_k.shape[1] // hh
    pgs = min(SB_PAGE_GROUP, npg)
    qs = (q * (dh ** -0.5)).astype(BF16).reshape(nb, t, hh, dh)
    eye = jnp.eye(hh, dtype=BF16)
    qbd = (qs.transpose(0, 2, 1, 3)[:, :, :, None, :] * eye[None, :, None, :, None]).reshape(nb, hh * t, w)
    bias_rep = jnp.broadcast_to(jnp.repeat(bias.astype(F32), t)[:, None], (hh * t, psz))
    pad = ((0, 0), (0, psz - t), (0, 0))
    knew, vnew = jnp.pad(k, pad), jnp.pad(v, pad)
    r = jnp.arange(psz)
    u = (r[:, None] >= r[None, :]).astype(BF16)
    u2 = jnp.concatenate([u, u], axis=0)
    per_seq = lambda b, p, pt: (b, 0, 0)
    cache = [pl.BlockSpec((None, psz * hh, dh), lambda b, p, pt, gi=gi: (pt[b, npg - 1 - (p * pgs + gi)], 0, 0))
             for gi in range(pgs)]
    return pl.pallas_call(
        functools.partial(_sbs_kernel, pages=pgs),
        grid_spec=pltpu.PrefetchScalarGridSpec(
            num_scalar_prefetch=1,
            grid=(nb, npg // pgs),
            in_specs=[pl.BlockSpec((None, hh * t, w), per_seq),
                      pl.BlockSpec((hh * t, psz), lambda b, p, pt: (0, 0)),
                      pl.BlockSpec((None, psz, w), per_seq),
                      pl.BlockSpec((None, psz, w), per_seq)]
                     + cache + cache
                     + [pl.BlockSpec((2 * psz, psz), lambda b, p, pt: (0, 0))],
            out_specs=pl.BlockSpec((None, t, w), per_seq),
            scratch_shapes=[pltpu.VMEM((hh * t, psz), F32), pltpu.VMEM((hh * t, w), F32)],
        ),
        out_shape=jax.ShapeDtypeStruct((nb, t, w), F32),
        compiler_params=_params("parallel", "arbitrary"),
        name="sb_sample",
    )(page_ids, qbd, bias_rep, knew, vnew, *([cache_k] * pgs), *([cache_v] * pgs), u2)


def _merge_kernel(oa_ref, ob_ref, zg_ref, ga_ref, gb_ref, x_ref, nw_ref, woa_ref, wob_ref, wout_ref,
                  g1_ref, b1_ref, h_ref, hb_ref, *, alpha):
    ob = ob_ref[...]
    zg = zg_ref[...]
    nw = nw_ref[...]
    parts = []
    for h in range(GDN_HEADS):
        sl = slice(h * GDN_DIM, (h + 1) * GDN_DIM)
        oh = ob[:, sl]
        zh = zg[:, sl]
        rn = oh * lax.rsqrt(jnp.mean(oh * oh, axis=-1, keepdims=True) + RMS_EPS)
        parts.append((rn * nw * (zh * jax.nn.sigmoid(zh))).astype(BF16))
    obn = jnp.concatenate(parts, axis=-1)
    merged = (jax.nn.sigmoid(ga_ref[...]) * _dot(oa_ref[...], woa_ref[...])
              + jax.nn.sigmoid(gb_ref[...]) * _dot(obn, wob_ref[...]))
    y = alpha * x_ref[...] + _dot(merged.astype(BF16), wout_ref[...])
    hn = _layernorm(y, g1_ref[...], b1_ref[...])
    h_ref[...] = hn
    hb_ref[...] = hn.astype(BF16)


def _merge(oa, ob, zg, gab, x, nw, woa, wob, wout, g1, b1, alpha, bm, name):
    m, d = x.shape
    w = oa.shape[1]
    bm = min(bm, m)
    row = lambda i: (i, 0)
    fix = lambda i: (0, 0)
    once = functools.partial(pl.BlockSpec, index_map=fix, pipeline_mode=pl.Buffered(1))
    return pl.pallas_call(
        functools.partial(_merge_kernel, alpha=alpha),
        grid=(m // bm,),
        in_specs=[pl.BlockSpec((bm, w), row), pl.BlockSpec((bm, w), row), pl.BlockSpec((bm, w), row),
                  pl.BlockSpec((bm, d), row), pl.BlockSpec((bm, d), lambda i: (i, 1)), pl.BlockSpec((bm, d), row),
                  pl.BlockSpec((1, GDN_DIM), fix),
                  once((w, d)), once((w, d)), once((d, d)),
                  pl.BlockSpec((1, d), fix), pl.BlockSpec((1, d), fix)],
        out_specs=[pl.BlockSpec((bm, d), row), pl.BlockSpec((bm, d), row)],
        out_shape=[jax.ShapeDtypeStruct((m, d), F32), jax.ShapeDtypeStruct((m, d), BF16)],
        compiler_params=_params("parallel"),
        name=name,
    )(oa, ob, zg, gab, gab, x, nw, woa, wob, wout, g1, b1)


def _ffn_kernel(hb_ref, h_ref, wg_ref, wu_ref, wd_ref, g2_ref, b2_ref, y_ref, acc_ref, *, alpha):
    f = pl.program_id(1)

    @pl.when(f == 0)
    def _():
        acc_ref[...] = jnp.zeros_like(acc_ref)

    hb = hb_ref[...]
    gate = _dot(hb, wg_ref[...])
    up = _dot(hb, wu_ref[...])
    act = (gate * jax.nn.sigmoid(gate) * up).astype(BF16)
    acc_ref[...] += _dot(act, wd_ref[...])

    @pl.when(f == pl.num_programs(1) - 1)
    def _():
        y_ref[...] = _layernorm(alpha * h_ref[...] + acc_ref[...], g2_ref[...], b2_ref[...])


def _ffn(hb, h, wgu, wd, g2, b2, alpha, bm, bf, name):
    m, d = h.shape
    dff = wd.shape[0]
    nf = dff // bf
    bm = min(bm, m)
    row = lambda i, f: (i, 0)
    fix = lambda i, f: (0, 0)
    return pl.pallas_call(
        functools.partial(_ffn_kernel, alpha=alpha),
        grid=(m // bm, nf),
        in_specs=[pl.BlockSpec((bm, d), row), pl.BlockSpec((bm, d), row),
                  pl.BlockSpec((d, bf), lambda i, f: (0, f)), pl.BlockSpec((d, bf), lambda i, f: (0, nf + f)),
                  pl.BlockSpec((bf, d), lambda i, f: (f, 0)),
                  pl.BlockSpec((1, d), fix), pl.BlockSpec((1, d), fix)],
        out_specs=pl.BlockSpec((bm, d), row),
        out_shape=jax.ShapeDtypeStruct((m, d), F32),
        scratch_shapes=[pltpu.VMEM((bm, d), F32)],
        compiler_params=_params("parallel", "arbitrary"),
        name=name,
    )(hb, h, wgu, wgu, wd, g2, b2)


def _layer(x, sb_mix, conv_prev, s_prev, wts, alpha, tag, bm):
    nb, t, d = x.shape
    m = nb * t
    w = SB_HEADS * SB_HEAD_DIM
    xf = x.reshape(m, d)
    cols = wts["in_cols"]

    def project(xb, nm):
        col0, n, bn = cols[nm]
        return _matmul(xb, wts["w_main"], bm, bn, f"proj_{nm}_{tag}", col0, n)

    xb, o_a, ka, va = sb_mix(xf, project)
    conv_in = project(xb, "conv").reshape(nb, t, 3 * w)
    zg, ba = project(xb, "z"), project(xb, "ba")
    gab = _matmul(xb, wts["w_gab"], bm, 1024, f"proj_gab_{tag}")

    prev8 = jnp.pad(conv_prev.astype(F32), ((0, 0), (8 - (CONV_WIDTH - 1), 0), (0, 0)))
    qn, kn, vv, bg = _gdn_pre(conv_in, prev8, wts["conv_w"], ba.reshape(nb, t, LANES), wts["gparams"],
                                   256, f"gdn_pre_{tag}")
    tp = -(-t // GDN_CHUNK) * GDN_CHUNK
    if tp != t:
        padt = lambda a: jnp.pad(a, ((0, 0), (0, tp - t), (0, 0)))
        qn, kn, vv, bg = map(padt, (qn, kn, vv, bg))
    o_b, s_new = _gdn(qn, kn, vv, bg, s_prev.astype(F32), tag)
    o_b = o_b[:, :t].reshape(m, w)

    h, hb = _merge(o_a, o_b, zg, gab, xf, wts["norm_w"], wts["w_o_a"], wts["w_o_b"], wts["w_out"],
                   wts["ln1_g"], wts["ln1_b"], alpha, 256, f"merge_{tag}")
    y = _ffn(hb, h, wts["w_gu"], wts["w_down"], wts["ln2_g"], wts["ln2_b"], alpha, 512, 512, f"ffn_{tag}")
    conv_new = jnp.concatenate([conv_prev.astype(F32), conv_in], axis=1)[:, -(CONV_WIDTH - 1):]
    return (y.reshape(nb, t, d),
            (ka.reshape(nb, t, SB_HEADS, SB_HEAD_DIM), va.reshape(nb, t, SB_HEADS, SB_HEAD_DIM), s_new, conv_new))


def _prep_weights(w_in, sb_bias, conv_w, a_log, dt_bias, gdn_norm_w, w_o_a, w_o_b, w_out,
                  ln1_g, ln1_b, w_gu, w_down, ln2_g, ln2_b):
    w = SB_HEADS * SB_HEAD_DIM
    d = w_in.shape[0]
    hh = GDN_HEADS
    o = 0
    in_cols = {}
    for nm, width, bn in (("q", w, w), ("k", w, w), ("v", w, w), ("conv", 3 * w, w), ("z", w, w),
                          ("ba", 2 * hh, LANES)):
        in_cols[nm] = (o, max(width, bn), bn)
        o += width
    o_gab = o
    assert in_cols["ba"][0] + LANES <= w_in.shape[1] and o_gab + 2 * d == w_in.shape[1]
    gp = jnp.zeros((8, LANES), F32)
    gp = gp.at[0, hh:2 * hh].set(-jnp.exp(a_log.astype(F32))).at[1, hh:2 * hh].set(dt_bias.astype(F32))
    w_main = w_in.astype(BF16)
    return dict(
        in_cols=in_cols, w_main=w_main, w_gab=w_main[:, o_gab:], w_q_t=w_main[:, :w].T,
        sb_bias=sb_bias, conv_w=conv_w.astype(F32), gparams=gp,
        norm_w=gdn_norm_w.astype(F32).reshape(1, GDN_DIM),
        w_o_a=w_o_a.astype(BF16), w_o_b=w_o_b.astype(BF16), w_out=w_out.astype(BF16),
        ln1_g=ln1_g.reshape(1, d), ln1_b=ln1_b.reshape(1, d),
        w_gu=w_gu.astype(BF16), w_down=w_down.astype(BF16),
        ln2_g=ln2_g.reshape(1, d), ln2_b=ln2_b.reshape(1, d))


def kernel(x_prompt, x_sample, cache_k, cache_v, state_gdn, state_conv, page_table, w_in, sb_bias, conv_w,
           a_log, dt_bias, gdn_norm_w, w_o_a, w_o_b, w_out, ln1_g, ln1_b, w_gu, w_down, ln2_g, ln2_b):
    depth = w_in.shape[0]
    alpha = (2.0 * depth) ** 0.25
    w = SB_HEADS * SB_HEAD_DIM
    nphys, psz = cache_k.shape[1], cache_k.shape[2]
    ck = cache_k.reshape(depth * nphys, psz * SB_HEADS, SB_HEAD_DIM)
    cv = cache_v.reshape(depth * nphys, psz * SB_HEADS, SB_HEAD_DIM)
    yp, ys = x_prompt, x_sample
    outs = [[] for _ in range(8)]
    for l in range(depth):
        wts = _prep_weights(w_in[l], sb_bias[l], conv_w[l], a_log[l], dt_bias[l], gdn_norm_w[l], w_o_a[l],
                            w_o_b[l], w_out[l], ln1_g[l], ln1_b[l], w_gu[l], w_down[l], ln2_g[l], ln2_b[l])
        nbp, tp = yp.shape[0], yp.shape[1]
        conv0 = jnp.zeros((nbp, CONV_WIDTH - 1, 3 * w), F32)
        s0 = jnp.zeros((nbp, GDN_HEADS, GDN_DIM, GDN_DIM), F32)

        assert nbp == 1, "prompt attention treats all prompt rows as one sequence"

        def sb_p(xf, project):
            k_col, v_col = (wts["in_cols"][nm][0] // w for nm in ("k", "v"))
            xb, qt, kf, kb, vf, vt = _attention_projections(xf, wts["w_q_t"], wts["w_main"], k_col, v_col, 1024, "p")
            return xb, _sb_prompt(qt.reshape(SB_HEADS, SB_HEAD_DIM, tp), kb, vt, wts["sb_bias"]), kf, vf

        yp, (kp, vp, sp, cp) = _layer(yp, sb_p, conv0, s0, wts, alpha, "p", 1024)

        nbs, ts = ys.shape[0], ys.shape[1]
        page_ids = page_table + l * nphys

        def sb_s(xf, project):
            xb = xf.astype(BF16)
            q, k, v = project(xb, "q"), project(xb, "k"), project(xb, "v")
            r3 = lambda a: a.reshape(nbs, ts, w)
            o = _sb_sample(r3(q), r3(k), r3(v), wts["sb_bias"], ck, cv, page_ids)
            return xb, o.reshape(nbs * ts, w).astype(BF16), k, v

        ys, (ksm, vsm, ssm, csm) = _layer(ys, sb_s, state_conv[l], state_gdn[l], wts, alpha, "s", 256)
        for lst, val in zip(outs, (kp, vp, sp, cp, ksm, vsm, ssm, csm)):
            lst.append(val)
    return (yp, ys) + tuple(jnp.stack(o) for o in outs)
```

```python
import functools

import jax
import jax.numpy as jnp
from jax import lax
from jax.experimental import pallas as pl
from jax.experimental.pallas import tpu as pltpu

F32 = jnp.float32
BF16 = jnp.bfloat16

LANES = 128
VMEM_LIMIT = 56 * 1024 * 1024

SB_HEADS = 8
SB_HEAD_DIM = 128
GDN_HEADS = 8
GDN_DIM = 128
CONV_WIDTH = 4
GDN_CHUNK = 64
LN_EPS = 1e-5
RMS_EPS = 1e-6
L2_EPS = 1e-6

SB_TQ = 256
SB_TK = 128
SB_HEAD_GROUP = 4
SB_KEY_BLOCKS = 32
SB_SKEW = 2
SB_PAGE_GROUP = 16
GDN_SPLIT_STEPS = 3
GDN_CHUNK_GROUP = 8
GDN_PREP_UNROLL = 4


def _params(*sem):
    return pltpu.CompilerParams(dimension_semantics=sem, vmem_limit_bytes=VMEM_LIMIT)


def _dot(a, b):
    return jnp.dot(a, b, preferred_element_type=F32)


def _dot_nt(a, b):
    return lax.dot_general(a, b, (((1,), (1,)), ((), ())), preferred_element_type=F32)


def _dot_tn(a, b):
    return lax.dot_general(a, b, (((0,), (0,)), ((), ())), preferred_element_type=F32)


def _split2(x):
    hi = x.astype(BF16)
    lo = (x - hi.astype(F32)).astype(BF16)
    return hi, lo


def _prefix_rows(x, rowid):
    d = 1
    while d < x.shape[0]:
        x = x + jnp.where(rowid >= d, pltpu.roll(x, d, 0), 0.0)
        d *= 2
    return x


def _neg_softplus(z):
    return jnp.minimum(-z, 0.0) - jnp.log(1.0 + jnp.exp(-jnp.abs(z)))


LOG2E = 1.4426950408889634


def _log2_one_minus_sigmoid(nz):
    nabs = pltpu.bitcast(pltpu.bitcast(nz, jnp.uint32) | jnp.uint32(0x80000000), F32)
    return jnp.minimum(nz, 0.0) - jnp.log(1.0 + jnp.exp2(nabs)) * LOG2E


def _softplus(z):
    return jnp.maximum(z, 0.0) + jnp.log1p(jnp.exp(-jnp.abs(z)))


def _layernorm(x, g, b):
    mu = jnp.mean(x, axis=-1, keepdims=True)
    xc = x - mu
    var = jnp.mean(xc * xc, axis=-1, keepdims=True)
    return xc * lax.rsqrt(var + LN_EPS) * g + b


def _mm_kernel(x_ref, w_ref, o_ref):
    o_ref[...] = _dot(x_ref[...], w_ref[...])


def _matmul(x, w, bm, bn, name, col0=0, n=None):
    m, k = x.shape
    n = w.shape[1] if n is None else n
    bm, bn = min(bm, m), min(bn, n)
    assert col0 % bn == 0 and n % bn == 0 and col0 + n <= w.shape[1]
    first = col0 // bn
    return pl.pallas_call(
        _mm_kernel,
        grid=(n // bn, m // bm),
        in_specs=[pl.BlockSpec((bm, k), lambda j, i: (i, 0)),
                  pl.BlockSpec((k, bn), lambda j, i: (0, first + j))],
        out_specs=pl.BlockSpec((bm, bn), lambda j, i: (i, j)),
        out_shape=jax.ShapeDtypeStruct((m, n), F32),
        compiler_params=_params("parallel", "parallel"),
        name=name,
    )(x, w)


def _mm_dual_kernel(x_ref, w_ref, o_ref, ob_ref):
    r = _dot(x_ref[...], w_ref[...])
    o_ref[...] = r
    ob_ref[...] = r.astype(BF16)


def _mm_nt_kernel(wt_ref, x_ref, o_ref, xb_ref, *, scale):
    xb = x_ref[...].astype(BF16)
    xb_ref[...] = xb
    o_ref[...] = (_dot_nt(wt_ref[...], xb) * scale).astype(BF16)


def _mm_vt_kernel(x_ref, w_ref, o_ref, vt_ref):
    r = _dot(x_ref[...], w_ref[...])
    o_ref[...] = r
    for h in range(vt_ref.shape[0]):
        for kb in range(vt_ref.shape[1]):
            tile = r[kb * SB_TK:(kb + 1) * SB_TK, h * SB_HEAD_DIM:(h + 1) * SB_HEAD_DIM]
            vt_ref[h, kb] = tile.T.astype(BF16)


def _attention_projections(x, wq_t, w, k_col, v_col, bm, tag):
    m, k = x.shape
    n = wq_t.shape[0]
    rows = pl.BlockSpec((bm, k), lambda i: (i, 0))
    out = pl.BlockSpec((bm, n), lambda i: (i, 0))
    par = _params("parallel")
    qt, xb = pl.pallas_call(
        functools.partial(_mm_nt_kernel, scale=-(SB_HEAD_DIM ** -0.5) * LOG2E),
        grid=(m // bm,),
        in_specs=[pl.BlockSpec((n, k), lambda i: (0, 0)), rows],
        out_specs=[pl.BlockSpec((n, bm), lambda i: (0, i)), rows],
        out_shape=[jax.ShapeDtypeStruct((n, m), BF16), jax.ShapeDtypeStruct((m, k), BF16)],
        compiler_params=par, name=f"proj_qt_{tag}",
    )(wq_t, x)
    kf, kb = pl.pallas_call(
        _mm_dual_kernel,
        grid=(m // bm,),
        in_specs=[rows, pl.BlockSpec((k, n), lambda i: (0, k_col))],
        out_specs=[out, out],
        out_shape=[jax.ShapeDtypeStruct((m, n), F32), jax.ShapeDtypeStruct((m, n), BF16)],
        compiler_params=par, name=f"proj_k_{tag}",
    )(xb, w)
    heads = n // SB_HEAD_DIM
    vf, vt = pl.pallas_call(
        _mm_vt_kernel,
        grid=(m // bm,),
        in_specs=[rows, pl.BlockSpec((k, n), lambda i: (0, v_col))],
        out_specs=[out, pl.BlockSpec((heads, bm // SB_TK, SB_HEAD_DIM, SB_TK), lambda i: (0, i, 0, 0))],
        out_shape=[jax.ShapeDtypeStruct((m, n), F32),
                   jax.ShapeDtypeStruct((heads, m // SB_TK, SB_HEAD_DIM, SB_TK), BF16)],
        compiler_params=par, name=f"proj_v_{tag}",
    )(xb, w)
    return xb, qt, kf, kb, vf, vt


def _gdn_pre_kernel(cur_ref, halo_ref, prev_ref, cw_ref, ba_ref, gp_ref, q_ref, k_ref, v_ref, bg_ref, ext_scr):
    j = pl.program_id(1)
    bm = cur_ref.shape[0]
    w = GDN_HEADS * GDN_DIM
    for blk in range(3 * GDN_HEADS):
        sl = slice(blk * GDN_DIM, (blk + 1) * GDN_DIM)
        cur = cur_ref[:, sl]
        ext_scr[0:8, :] = jnp.where(j == 0, prev_ref[:, sl], halo_ref[:, sl])
        ext_scr[8:, :] = cur
        conv = cur * cw_ref[CONV_WIDTH - 1:CONV_WIDTH, sl]
        for d in range(1, CONV_WIDTH):
            conv = conv + ext_scr[pl.ds(8 - d, bm), :] * cw_ref[CONV_WIDTH - 1 - d:CONV_WIDTH - d, sl]
        c = conv * jax.nn.sigmoid(conv)
        osl = slice(sl.start % w, sl.start % w + GDN_DIM)
        if blk < GDN_HEADS:
            q_ref[:, osl] = c * lax.rsqrt(jnp.sum(c * c, axis=-1, keepdims=True) + L2_EPS) * (GDN_DIM ** -0.5)
        elif blk < 2 * GDN_HEADS:
            k_ref[:, osl] = c * lax.rsqrt(jnp.sum(c * c, axis=-1, keepdims=True) + L2_EPS)
        else:
            v_ref[:, osl] = c
    ba = ba_ref[...]
    gp = gp_ref[...]
    lane = lax.broadcasted_iota(jnp.int32, ba.shape, 1)
    bg_ref[...] = jnp.where(lane < GDN_HEADS, jax.nn.sigmoid(ba), gp[0:1] * _softplus(ba + gp[1:2]))


def _gdn_pre(conv_in, conv_prev8, conv_w, ba, gparams, bm, name):
    nb, t, c3 = conv_in.shape
    w = c3 // 3
    bm = min(bm, t)
    hb = bm // 8
    row = lambda b, j: (b, j, 0)
    out = jax.ShapeDtypeStruct((nb, t, w), F32)
    return pl.pallas_call(
        _gdn_pre_kernel,
        grid=(nb, t // bm),
        in_specs=[pl.BlockSpec((None, bm, c3), row),
                  pl.BlockSpec((None, 8, c3), lambda b, j: (b, jnp.maximum(j * hb - 1, 0), 0)),
                  pl.BlockSpec((None, 8, c3), lambda b, j: (b, 0, 0)),
                  pl.BlockSpec((CONV_WIDTH, c3), lambda b, j: (0, 0)),
                  pl.BlockSpec((None, bm, LANES), row),
                  pl.BlockSpec((8, LANES), lambda b, j: (0, 0))],
        out_specs=[pl.BlockSpec((None, bm, w), row)] * 3 + [pl.BlockSpec((None, bm, LANES), row)],
        out_shape=[out] * 3 + [jax.ShapeDtypeStruct((nb, t, LANES), F32)],
        scratch_shapes=[pltpu.VMEM((bm + 8, GDN_DIM), F32)],
        compiler_params=_params("parallel", "parallel"),
        name=name,
    )(conv_in, conv_in, conv_prev8, conv_w, ba, gparams)


def _gdn_prep_kernel(q_ref, k_ref, v_ref, bg_ref, u_ref, w_ref, qd_ref, kd_ref, intra_ref, gl_ref,
                     *, chunks):
    n = GDN_CHUNK
    ri = lax.broadcasted_iota(jnp.int32, (n, n), 0)
    ci = lax.broadcasted_iota(jnp.int32, (n, n), 1)
    incl = ri >= ci
    strict = ri > ci
    lane_pad = jnp.zeros((n, GDN_DIM - n), F32)
    rowid = lax.broadcasted_iota(jnp.int32, (n, GDN_DIM), 0)
    unroll = GDN_PREP_UNROLL if chunks % GDN_PREP_UNROLL == 0 else 1

    def chunk(cc, carry):
        cs = [cc * unroll + d for d in range(unroll)]
        hs = range(GDN_HEADS * unroll)
        sls = [slice((h % GDN_HEADS) * GDN_DIM, (h % GDN_HEADS + 1) * GDN_DIM) for h in hs]
        rws = [pl.ds(pl.multiple_of(cs[h // GDN_HEADS] * n, n), n) for h in hs]
        ks = [k_ref[r, sl] for r, sl in zip(rws, sls)]
        bgs = [bg_ref[r, :] for r in rws[::GDN_HEADS]]
        lanes = lambda col: jnp.broadcast_to(col, (n, GDN_DIM))
        betas = [lanes(bgs[h // GDN_HEADS][:, h % GDN_HEADS:h % GDN_HEADS + 1]) for h in hs]
        gs = [lanes(bgs[h // GDN_HEADS][:, GDN_HEADS + h % GDN_HEADS:GDN_HEADS + h % GDN_HEADS + 1]) for h in hs]
        gcums = [_prefix_rows(g, rowid) for g in gs]
        grows = [jnp.broadcast_to(gc.T[0:1, :], (n, n)) for gc in gcums]
        kbfs = [k.astype(BF16) for k in ks]
        kbs = [k * beta for k, beta in zip(ks, betas)]
        kks = [_dot_nt(kb.astype(BF16), kbf) for kb, kbf in zip(kbs, kbfs)]
        qks = [_dot_nt(q_ref[r, sl].astype(BF16), kbf) for r, sl, kbf in zip(rws, sls, kbfs)]
        gams = [jnp.where(incl, jnp.exp(jnp.where(incl, gc[:, :n] - gr, 0.0)), 0.0) for gc, gr in zip(gcums, grows)]
        egs = [jnp.exp(gc) for gc in gcums]
        mps = [-jnp.where(strict, kk * gam, 0.0) for kk, gam in zip(kks, gams)]
        sols = [jnp.concatenate([v_ref[r, sl] * beta, kb * eg], axis=-1)
                for r, sl, beta, kb, eg in zip(rws, sls, betas, kbs, egs)]
        span = 1
        while span < n:
            mpbs = [mp.astype(BF16) for mp in mps]
            if span < (1 << GDN_SPLIT_STEPS):
                upds = [_dot(mpb, jnp.concatenate(_split2(sol), axis=-1)) for mpb, sol in zip(mpbs, sols)]
                sols = [sol + (upd[:, :2 * GDN_DIM] + upd[:, 2 * GDN_DIM:]) for sol, upd in zip(sols, upds)]
            else:
                sols = [sol + _dot(mpb, sol.astype(BF16)) for mpb, sol in zip(mpbs, sols)]
            span *= 2
            if span < n:
                mps = [_dot(mpb, mpb) for mpb in mpbs]
        for h in hs:
            sl, rows, head = sls[h], rws[h], h % GDN_HEADS
            glast = gcums[h][n - 1:n, :]
            u_ref[rows, sl] = sols[h][:, :GDN_DIM]
            w_ref[rows, sl] = sols[h][:, GDN_DIM:].astype(BF16)
            intra = jnp.where(incl, qks[h] * gams[h], 0.0)
            intra_ref[rows, sl] = jnp.concatenate([intra, lane_pad], axis=-1).astype(BF16)
            qd_ref[rows, sl] = (q_ref[rows, sl] * egs[h]).astype(BF16)
            kd_ref[rows, sl] = (ks[h] * jnp.exp(glast - gcums[h])).astype(BF16)
            gl_ref[cs[h // GDN_HEADS], head:head + 1, :] = jnp.exp(glast)
        return carry

    lax.fori_loop(0, chunks // unroll, chunk, 0)


def _gdn_scan_kernel(u_ref, w_ref, qd_ref, kd_ref, intra_ref, gl_ref, s0_ref, o_ref, sfin_ref, s_scr, *, chunks):
    n = GDN_CHUNK
    cg = pl.program_id(1)

    @pl.when(cg == 0)
    def _():
        s_scr[...] = s0_ref[...]

    def chunk(c, carry):
        rows = pl.ds(pl.multiple_of(c * n, n), n)
        hs = range(GDN_HEADS)
        sls = [slice(h * GDN_DIM, (h + 1) * GDN_DIM) for h in hs]
        ss = [s_scr[h] for h in hs]
        rs = [_dot(jnp.concatenate([w_ref[rows, sl], qd_ref[rows, sl]], axis=0), s.astype(BF16))
              for sl, s in zip(sls, ss)]
        vbs = [(u_ref[rows, sl] - r[:n]).astype(BF16) for sl, r in zip(sls, rs)]
        upds = [_dot_tn(kd_ref[rows, sl], vb) for sl, vb in zip(sls, vbs)]
        for h in hs:
            s_scr[h] = ss[h] * gl_ref[c, h:h + 1, :] + upds[h]
        for h in hs:
            o_ref[rows, sls[h]] = rs[h][n:] + _dot(intra_ref[rows, h * GDN_DIM:h * GDN_DIM + n], vbs[h])
        return carry

    lax.fori_loop(0, chunks, chunk, 0)

    @pl.when(cg == pl.num_programs(1) - 1)
    def _():
        sfin_ref[...] = s_scr[...]


def _gdn(q, k, v, bg, s0, tag):
    nb, t, w = q.shape
    nc = t // GDN_CHUNK
    cgs = min(GDN_CHUNK_GROUP, nc)
    rows = cgs * GDN_CHUNK
    tok = pl.BlockSpec((None, rows, w), lambda b, c: (b, c, 0))
    gl = pl.BlockSpec((None, cgs, GDN_HEADS, GDN_DIM), lambda b, c: (b, c, 0, 0))
    st = pl.BlockSpec((None, GDN_HEADS, GDN_DIM, GDN_DIM), lambda b, c: (b, 0, 0, 0))
    tok_f32 = jax.ShapeDtypeStruct((nb, t, w), F32)
    tok_bf16 = jax.ShapeDtypeStruct((nb, t, w), BF16)
    gl_shape = jax.ShapeDtypeStruct((nb, nc, GDN_HEADS, GDN_DIM), F32)
    u, wy, qd, kd, intra, glast = pl.pallas_call(
        functools.partial(_gdn_prep_kernel, chunks=cgs),
        grid=(nb, nc // cgs),
        in_specs=[tok] * 3 + [pl.BlockSpec((None, rows, LANES), lambda b, c: (b, c, 0))],
        out_specs=[tok] * 5 + [gl],
        out_shape=[tok_f32] + [tok_bf16] * 4 + [gl_shape],
        compiler_params=_params("parallel", "parallel"),
        name=f"gdn_prep_{tag}",
    )(q, k, v, bg)
    return pl.pallas_call(
        functools.partial(_gdn_scan_kernel, chunks=cgs),
        grid=(nb, nc // cgs),
        in_specs=[tok] * 5 + [gl, st],
        out_specs=[tok, st],
        out_shape=[tok_f32, jax.ShapeDtypeStruct((nb, GDN_HEADS, GDN_DIM, GDN_DIM), F32)],
        scratch_shapes=[pltpu.VMEM((GDN_HEADS, GDN_DIM, GDN_DIM), F32)],
        compiler_params=_params("parallel", "arbitrary"),
        name=f"gdn_scan_{tag}",
    )(u, wy, qd, kd, intra, glast, s0)


def _sbp_kernel(bias_ref, qt_ref, k_ref, vt_ref, u2_ref, o_ref, acc_ref):
    hg = pl.program_id(0)
    i = pl.program_id(1)
    nh = qt_ref.shape[0]
    u2 = u2_ref[...]
    tk, tq = SB_TK, SB_TQ

    def blocks(js, accs, offs):
        valid = [None if o is None else
                 (lax.broadcasted_iota(jnp.int32, (tk, tq), 0) + o) < lax.broadcasted_iota(jnp.int32, (tk, tq), 1)
                 for o in offs]
        chains = [(b, h) for b in range(len(js)) for h in range(nh)]
        rows = [pl.ds(pl.multiple_of(j * tk, tk), tk) for j in js]
        accs = list(accs)
        nz, lg, rinc = {}, {}, {}

        def st_qk(c):
            b, h = c
            nz[c] = _dot(k_ref[rows[b], h * SB_HEAD_DIM:(h + 1) * SB_HEAD_DIM], qt_ref[h]) + bias_ref[hg * nh + h]

        def st_lg(c):
            v = _log2_one_minus_sigmoid(nz[c])
            lg[c] = v if valid[c[0]] is None else jnp.where(valid[c[0]], v, 0.0)

        def st_sum(c):
            rinc[c] = _dot(u2, jnp.concatenate(_split2(lg.pop(c)), axis=0))

        def st_av(c):
            b, h = c
            a = jnp.exp2((rinc[c] + accs[h]) - nz.pop(c))
            if valid[b] is not None:
                a = jnp.where(valid[b], a, 0.0)
            acc_ref[h] += _dot(vt_ref[h, js[b]], a.astype(BF16))
            accs[h] = accs[h] + rinc.pop(c)[0:1, :]

        stages = [st_qk, st_lg, st_sum, st_av]
        for step in range(len(chains) + (len(stages) - 1) * SB_SKEW):
            for si, st in enumerate(stages):
                ci = step - si * SB_SKEW
                if 0 <= ci < len(chains):
                    st(chains[ci])
        return tuple(accs)

    acc_ref[...] = jnp.zeros_like(acc_ref)
    per = tq // tk
    nb = SB_KEY_BLOCKS
    first = i * per
    accs = blocks([first + per - 1 - d for d in range(per)], tuple(jnp.zeros((1, tq), F32) for _ in range(nh)),
                  [(per - 1 - d) * tk for d in range(per)])
    accs = lax.fori_loop(0, first // nb,
                         lambda n, c: blocks([first - 1 - nb * n - d for d in range(nb)], c, [None] * nb), accs)
    done = (first // nb) * nb
    size = nb // 2
    while size >= per:
        take = ((first - done) // size) % 2
        start = first - done - 1
        accs = lax.fori_loop(0, take, lambda n, c, start=start, size=size:
                             blocks([start - d for d in range(size)], c, [None] * size), accs)
        done = done + take * size
        size //= 2
    for h in range(nh):
        o_ref[:, h * SB_HEAD_DIM:(h + 1) * SB_HEAD_DIM] = acc_ref[h].T.astype(BF16)


def _sb_prompt(qt, kb, vt, bias):
    hh, dh, s = qt.shape
    hg = SB_HEAD_GROUP
    nkb = s // SB_TK
    r = jnp.arange(SB_TK)
    u = (r[None, :] >= r[:, None]).astype(BF16)
    u2 = jnp.concatenate([u, u], axis=1)
    once = pl.Buffered(1)
    return pl.pallas_call(
        _sbp_kernel,
        grid=(hh // hg, s // SB_TQ),
        in_specs=[pl.BlockSpec(memory_space=pltpu.SMEM),
                  pl.BlockSpec((hg, dh, SB_TQ), lambda g, i: (g, 0, i)),
                  pl.BlockSpec((s, hg * dh), lambda g, i: (0, g), pipeline_mode=once),
                  pl.BlockSpec((hg, nkb, dh, SB_TK), lambda g, i: (g, 0, 0, 0), pipeline_mode=once),
                  pl.BlockSpec((SB_TK, 2 * SB_TK), lambda g, i: (0, 0), pipeline_mode=once)],
        out_specs=pl.BlockSpec((SB_TQ, hg * dh), lambda g, i: (i, g)),
        out_shape=jax.ShapeDtypeStruct((s, hh * dh), BF16),
        scratch_shapes=[pltpu.VMEM((hg, dh, SB_TQ), F32)],
        compiler_params=_params("parallel", "arbitrary"),
        name="sb_prompt",
    )(bias.astype(F32) * (-LOG2E), qt, kb, vt, u2)


def _sbs_kernel(pt_ref, qbd_ref, bias_ref, knew_ref, vnew_ref, *rest, pages):
    kc_refs, vc_refs = rest[:pages], rest[pages:2 * pages]
    u2_ref, o_ref, acc_scr, out_scr = rest[2 * pages:]
    p = pl.program_id(1)
    qbd = qbd_ref[...]
    bias = bias_ref[...]
    u2 = u2_ref[...]
    nrow, npos = bias.shape
    t = nrow // SB_HEADS

    def page_rows(ref):
        if ref.shape[1] != SB_HEAD_DIM:
            return ref[...].astype(BF16)
        return jnp.concatenate([ref[pl.ds(h, npos, stride=SB_HEADS), :].astype(BF16) for h in range(SB_HEADS)],
                               axis=1)

    def group(k_refs, v_refs, masked):
        kcat = jnp.concatenate([page_rows(r) for r in k_refs], axis=0)
        vcat = jnp.concatenate([page_rows(r) for r in v_refs], axis=0)
        zall = _dot_nt(qbd, kcat)
        acc = acc_scr[...]
        probs = []
        for gi in range(len(k_refs)):
            z = zall[:, gi * npos:(gi + 1) * npos] + bias
            lg = _neg_softplus(z)
            if masked:
                tok = lax.broadcasted_iota(jnp.int32, (nrow, npos), 0) % t
                valid = lax.broadcasted_iota(jnp.int32, (nrow, npos), 1) < tok
                lg = jnp.where(valid, lg, 0.0)
            hi, lo = _split2(lg)
            rinc = _dot(jnp.concatenate([hi, lo], axis=1), u2)
            a = jnp.exp(z + rinc + acc)
            if masked:
                a = jnp.where(valid, a, 0.0)
            probs.append(a.astype(BF16))
            acc = acc + jnp.sum(lg, axis=-1, keepdims=True)
        out_scr[...] += _dot(jnp.concatenate(probs, axis=1), vcat)
        acc_scr[...] = acc

    @pl.when(p == 0)
    def _():
        acc_scr[...] = jnp.zeros_like(acc_scr)
        out_scr[...] = jnp.zeros_like(out_scr)
        group([knew_ref], [vnew_ref], True)

    group(kc_refs, vc_refs, False)

    @pl.when(p == pl.num_programs(1) - 1)
    def _():
        for h in range(SB_HEADS):
            sl = slice(h * SB_HEAD_DIM, (h + 1) * SB_HEAD_DIM)
            o_ref[:, sl] = out_scr[h * t:(h + 1) * t, sl]


def _sb_sample(q, k, v, bias, cache_k, cache_v, page_ids):
    nb, t, w = q.shape
    hh, dh = SB_HEADS, SB_HEAD_DIM
    npg = page_ids.shape[1]
    psz = cache_k.shape[1] // hh
    pgs = min(SB_PAGE_GROUP, npg)
    qs = (q * (dh ** -0.5)).astype(BF16).reshape(nb, t, hh, dh)
    eye = jnp.eye(hh, dtype=BF16)
    qbd = (qs.transpose(0, 2, 1, 3)[:, :, :, None, :] * eye[None, :, None, :, None]).reshape(nb, hh * t, w)
    bias_rep = jnp.broadcast_to(jnp.repeat(bias.astype(F32), t)[:, None], (hh * t, psz))
    pad = ((0, 0), (0, psz - t), (0, 0))
    knew, vnew = jnp.pad(k, pad), jnp.pad(v, pad)
    r = jnp.arange(psz)
    u = (r[:, None] >= r[None, :]).astype(BF16)
    u2 = jnp.concatenate([u, u], axis=0)
    per_seq = lambda b, p, pt: (b, 0, 0)
    cache = [pl.BlockSpec((None, psz * hh, dh), lambda b, p, pt, gi=gi: (pt[b, npg - 1 - (p * pgs + gi)], 0, 0))
             for gi in range(pgs)]
    return pl.pallas_call(
        functools.partial(_sbs_kernel, pages=pgs),
        grid_spec=pltpu.PrefetchScalarGridSpec(
            num_scalar_prefetch=1,
            grid=(nb, npg // pgs),
            in_specs=[pl.BlockSpec((None, hh * t, w), per_seq),
                      pl.BlockSpec((hh * t, psz), lambda b, p, pt: (0, 0)),
                      pl.BlockSpec((None, psz, w), per_seq),
                      pl.BlockSpec((None, psz, w), per_seq)]
                     + cache + cache
                     + [pl.BlockSpec((2 * psz, psz), lambda b, p, pt: (0, 0))],
            out_specs=pl.BlockSpec((None, t, w), per_seq),
            scratch_shapes=[pltpu.VMEM((hh * t, psz), F32), pltpu.VMEM((hh * t, w), F32)],
        ),
        out_shape=jax.ShapeDtypeStruct((nb, t, w), F32),
        compiler_params=_params("parallel", "arbitrary"),
        name="sb_sample",
    )(page_ids, qbd, bias_rep, knew, vnew, *([cache_k] * pgs), *([cache_v] * pgs), u2)


def _merge_kernel(oa_ref, ob_ref, zg_ref, ga_ref, gb_ref, x_ref, nw_ref, woa_ref, wob_ref, wout_ref,
                  g1_ref, b1_ref, h_ref, hb_ref, *, alpha):
    ob = ob_ref[...]
    zg = zg_ref[...]
    nw = nw_ref[...]
    parts = []
    for h in range(GDN_HEADS):
        sl = slice(h * GDN_DIM, (h + 1) * GDN_DIM)
        oh = ob[:, sl]
        zh = zg[:, sl]
        rn = oh * lax.rsqrt(jnp.mean(oh * oh, axis=-1, keepdims=True) + RMS_EPS)
        parts.append((rn * nw * (zh * jax.nn.sigmoid(zh))).astype(BF16))
    obn = jnp.concatenate(parts, axis=-1)
    merged = (jax.nn.sigmoid(ga_ref[...]) * _dot(oa_ref[...], woa_ref[...])
              + jax.nn.sigmoid(gb_ref[...]) * _dot(obn, wob_ref[...]))
    y = alpha * x_ref[...] + _dot(merged.astype(BF16), wout_ref[...])
    hn = _layernorm(y, g1_ref[...], b1_ref[...])
    h_ref[...] = hn
    hb_ref[...] = hn.astype(BF16)


def _merge(oa, ob, zg, gab, x, nw, woa, wob, wout, g1, b1, alpha, bm, name):
    m, d = x.shape
    w = oa.shape[1]
    bm = min(bm, m)
    row = lambda i: (i, 0)
    fix = lambda i: (0, 0)
    once = functools.partial(pl.BlockSpec, index_map=fix, pipeline_mode=pl.Buffered(1))
    return pl.pallas_call(
        functools.partial(_merge_kernel, alpha=alpha),
        grid=(m // bm,),
        in_specs=[pl.BlockSpec((bm, w), row), pl.BlockSpec((bm, w), row), pl.BlockSpec((bm, w), row),
                  pl.BlockSpec((bm, d), row), pl.BlockSpec((bm, d), lambda i: (i, 1)), pl.BlockSpec((bm, d), row),
                  pl.BlockSpec((1, GDN_DIM), fix),
                  once((w, d)), once((w, d)), once((d, d)),
                  pl.BlockSpec((1, d), fix), pl.BlockSpec((1, d), fix)],
        out_specs=[pl.BlockSpec((bm, d), row), pl.BlockSpec((bm, d), row)],
        out_shape=[jax.ShapeDtypeStruct((m, d), F32), jax.ShapeDtypeStruct((m, d), BF16)],
        compiler_params=_params("parallel"),
        name=name,
    )(oa, ob, zg, gab, gab, x, nw, woa, wob, wout, g1, b1)


def _ffn_kernel(hb_ref, h_ref, wg_ref, wu_ref, wd_ref, g2_ref, b2_ref, y_ref, acc_ref, *, alpha):
    f = pl.program_id(1)

    @pl.when(f == 0)
    def _():
        acc_ref[...] = jnp.zeros_like(acc_ref)

    hb = hb_ref[...]
    gate = _dot(hb, wg_ref[...])
    up = _dot(hb, wu_ref[...])
    act = (gate * jax.nn.sigmoid(gate) * up).astype(BF16)
    acc_ref[...] += _dot(act, wd_ref[...])

    @pl.when(f == pl.num_programs(1) - 1)
    def _():
        y_ref[...] = _layernorm(alpha * h_ref[...] + acc_ref[...], g2_ref[...], b2_ref[...])


def _ffn(hb, h, wgu, wd, g2, b2, alpha, bm, bf, name):
    m, d = h.shape
    dff = wd.shape[0]
    nf = dff // bf
    bm = min(bm, m)
    row = lambda i, f: (i, 0)
    fix = lambda i, f: (0, 0)
    return pl.pallas_call(
        functools.partial(_ffn_kernel, alpha=alpha),
        grid=(m // bm, nf),
        in_specs=[pl.BlockSpec((bm, d), row), pl.BlockSpec((bm, d), row),
                  pl.BlockSpec((d, bf), lambda i, f: (0, f)), pl.BlockSpec((d, bf), lambda i, f: (0, nf + f)),
                  pl.BlockSpec((bf, d), lambda i, f: (f, 0)),
                  pl.BlockSpec((1, d), fix), pl.BlockSpec((1, d), fix)],
        out_specs=pl.BlockSpec((bm, d), row),
        out_shape=jax.ShapeDtypeStruct((m, d), F32),
        scratch_shapes=[pltpu.VMEM((bm, d), F32)],
        compiler_params=_params("parallel", "arbitrary"),
        name=name,
    )(hb, h, wgu, wgu, wd, g2, b2)


def _layer(x, sb_mix, conv_prev, s_prev, wts, alpha, tag, bm):
    nb, t, d = x.shape
    m = nb * t
    w = SB_HEADS * SB_HEAD_DIM
    xf = x.reshape(m, d)
    cols = wts["in_cols"]

    def project(xb, nm):
        col0, n, bn = cols[nm]
        return _matmul(xb, wts["w_main"], bm, bn, f"proj_{nm}_{tag}", col0, n)

    xb, o_a, ka, va = sb_mix(xf, project)
    conv_in = project(xb, "conv").reshape(nb, t, 3 * w)
    zg, ba = project(xb, "z"), project(xb, "ba")
    gab = _matmul(xb, wts["w_gab"], bm, 1024, f"proj_gab_{tag}")

    prev8 = jnp.pad(conv_prev.astype(F32), ((0, 0), (8 - (CONV_WIDTH - 1), 0), (0, 0)))
    qn, kn, vv, bg = _gdn_pre(conv_in, prev8, wts["conv_w"], ba.reshape(nb, t, LANES), wts["gparams"],
                                   256, f"gdn_pre_{tag}")
    tp = -(-t // GDN_CHUNK) * GDN_CHUNK
    if tp != t:
        padt = lambda a: jnp.pad(a, ((0, 0), (0, tp - t), (0, 0)))
        qn, kn, vv, bg = map(padt, (qn, kn, vv, bg))
    o_b, s_new = _gdn(qn, kn, vv, bg, s_prev.astype(F32), tag)
    o_b = o_b[:, :t].reshape(m, w)

    h, hb = _merge(o_a, o_b, zg, gab, xf, wts["norm_w"], wts["w_o_a"], wts["w_o_b"], wts["w_out"],
                   wts["ln1_g"], wts["ln1_b"], alpha, 256, f"merge_{tag}")
    y = _ffn(hb, h, wts["w_gu"], wts["w_down"], wts["ln2_g"], wts["ln2_b"], alpha, 512, 512, f"ffn_{tag}")
    conv_new = jnp.concatenate([conv_prev.astype(F32), conv_in], axis=1)[:, -(CONV_WIDTH - 1):]
    return (y.reshape(nb, t, d),
            (ka.reshape(nb, t, SB_HEADS, SB_HEAD_DIM), va.reshape(nb, t, SB_HEADS, SB_HEAD_DIM), s_new, conv_new))


def _prep_weights(w_in, sb_bias, conv_w, a_log, dt_bias, gdn_norm_w, w_o_a, w_o_b, w_out,
                  ln1_g, ln1_b, w_gu, w_down, ln2_g, ln2_b):
    w = SB_HEADS * SB_HEAD_DIM
    d = w_in.shape[0]
    hh = GDN_HEADS
    o = 0
    in_cols = {}
    for nm, width, bn in (("q", w, w), ("k", w, w), ("v", w, w), ("conv", 3 * w, w), ("z", w, w),
                          ("ba", 2 * hh, LANES)):
        in_cols[nm] = (o, max(width, bn), bn)
        o += width
    o_gab = o
    assert in_cols["ba"][0] + LANES <= w_in.shape[1] and o_gab + 2 * d == w_in.shape[1]
    gp = jnp.zeros((8, LANES), F32)
    gp = gp.at[0, hh:2 * hh].set(-jnp.exp(a_log.astype(F32))).at[1, hh:2 * hh].set(dt_bias.astype(F32))
    w_main = w_in.astype(BF16)
    return dict(
        in_cols=in_cols, w_main=w_main, w_gab=w_main[:, o_gab:], w_q_t=w_main[:, :w].T,
        sb_bias=sb_bias, conv_w=conv_w.astype(F32), gparams=gp,
        norm_w=gdn_norm_w.astype(F32).reshape(1, GDN_DIM),
        w_o_a=w_o_a.astype(BF16), w_o_b=w_o_b.astype(BF16), w_out=w_out.astype(BF16),
        ln1_g=ln1_g.reshape(1, d), ln1_b=ln1_b.reshape(1, d),
        w_gu=w_gu.astype(BF16), w_down=w_down.astype(BF16),
        ln2_g=ln2_g.reshape(1, d), ln2_b=ln2_b.reshape(1, d))


def kernel(x_prompt, x_sample, cache_k, cache_v, state_gdn, state_conv, page_table, w_in, sb_bias, conv_w,
           a_log, dt_bias, gdn_norm_w, w_o_a, w_o_b, w_out, ln1_g, ln1_b, w_gu, w_down, ln2_g, ln2_b):
    depth = w_in.shape[0]
    alpha = (2.0 * depth) ** 0.25
    w = SB_HEADS * SB_HEAD_DIM
    nphys, psz = cache_k.shape[1], cache_k.shape[2]
    ck = cache_k.reshape(depth * nphys, psz * SB_HEADS, SB_HEAD_DIM)
    cv = cache_v.reshape(depth * nphys, psz * SB_HEADS, SB_HEAD_DIM)
    yp, ys = x_prompt, x_sample
    outs = [[] for _ in range(8)]
    for l in range(depth):
        wts = _prep_weights(w_in[l], sb_bias[l], conv_w[l], a_log[l], dt_bias[l], gdn_norm_w[l], w_o_a[l],
                            w_o_b[l], w_out[l], ln1_g[l], ln1_b[l], w_gu[l], w_down[l], ln2_g[l], ln2_b[l])
        nbp, tp = yp.shape[0], yp.shape[1]
        conv0 = jnp.zeros((nbp, CONV_WIDTH - 1, 3 * w), F32)
        s0 = jnp.zeros((nbp, GDN_HEADS, GDN_DIM, GDN_DIM), F32)

        assert nbp == 1, "prompt attention treats all prompt rows as one sequence"

        def sb_p(xf, project):
            k_col, v_col = (wts["in_cols"][nm][0] // w for nm in ("k", "v"))
            xb, qt, kf, kb, vf, vt = _attention_projections(xf, wts["w_q_t"], wts["w_main"], k_col, v_col, 1024, "p")
            return xb, _sb_prompt(qt.reshape(SB_HEADS, SB_HEAD_DIM, tp), kb, vt, wts["sb_bias"]), kf, vf

        yp, (kp, vp, sp, cp) = _layer(yp, sb_p, conv0, s0, wts, alpha, "p", 1024)

        nbs, ts = ys.shape[0], ys.shape[1]
        page_ids = page_table + l * nphys

        def sb_s(xf, project):
            xb = xf.astype(BF16)
            q, k, v = project(xb, "q"), project(xb, "k"), project(xb, "v")
            r3 = lambda a: a.reshape(nbs, ts, w)
            o = _sb_sample(r3(q), r3(k), r3(v), wts["sb_bias"], ck, cv, page_ids)
            return xb, o.reshape(nbs * ts, w).astype(BF16), k, v

        ys, (ksm, vsm, ssm, csm) = _layer(ys, sb_s, state_conv[l], state_gdn[l], wts, alpha, "s", 256)
        for lst, val in zip(outs, (kp, vp, sp, cp, ksm, vsm, ssm, csm)):
            lst.append(val)
    return (yp, ys) + tuple(jnp.stack(o) for o in outs)
```

```python
import functools

import jax
import jax.numpy as jnp
from jax import lax
from jax.experimental import pallas as pl
from jax.experimental.pallas import tpu as pltpu

F32 = jnp.float32
BF16 = jnp.bfloat16

LANES = 128
VMEM_LIMIT = 56 * 1024 * 1024

SB_HEADS = 8
SB_HEAD_DIM = 128
GDN_HEADS = 8
GDN_DIM = 128
CONV_WIDTH = 4
GDN_CHUNK = 64
LN_EPS = 1e-5
RMS_EPS = 1e-6
L2_EPS = 1e-6

SB_TQ = 256
SB_TK = 128
SB_HEAD_GROUP = 4
SB_KEY_BLOCKS = 32
SB_SKEW = 2
SB_PAGE_GROUP = 16
GDN_SPLIT_STEPS = 3
GDN_CHUNK_GROUP = 8
GDN_PREP_UNROLL = 4


def _params(*sem):
    return pltpu.CompilerParams(dimension_semantics=sem, vmem_limit_bytes=VMEM_LIMIT)


def _dot(a, b):
    return jnp.dot(a, b, preferred_element_type=F32)


def _dot_nt(a, b):
    return lax.dot_general(a, b, (((1,), (1,)), ((), ())), preferred_element_type=F32)


def _dot_tn(a, b):
    return lax.dot_general(a, b, (((0,), (0,)), ((), ())), preferred_element_type=F32)


def _split2(x):
    hi = x.astype(BF16)
    lo = (x - hi.astype(F32)).astype(BF16)
    return hi, lo


def _prefix_rows(x, rowid):
    d = 1
    while d < x.shape[0]:
        x = x + jnp.where(rowid >= d, pltpu.roll(x, d, 0), 0.0)
        d *= 2
    return x


def _neg_softplus(z):
    return jnp.minimum(-z, 0.0) - jnp.log(1.0 + jnp.exp(-jnp.abs(z)))


LOG2E = 1.4426950408889634


def _log2_one_minus_sigmoid(nz):
    nabs = pltpu.bitcast(pltpu.bitcast(nz, jnp.uint32) | jnp.uint32(0x80000000), F32)
    return jnp.minimum(nz, 0.0) - jnp.log(1.0 + jnp.exp2(nabs)) * LOG2E


def _softplus(z):
    return jnp.maximum(z, 0.0) + jnp.log1p(jnp.exp(-jnp.abs(z)))


def _layernorm(x, g, b):
    mu = jnp.mean(x, axis=-1, keepdims=True)
    xc = x - mu
    var = jnp.mean(xc * xc, axis=-1, keepdims=True)
    return xc * lax.rsqrt(var + LN_EPS) * g + b


def _mm_kernel(x_ref, w_ref, o_ref):
    o_ref[...] = _dot(x_ref[...], w_ref[...])


def _matmul(x, w, bm, bn, name, col0=0, n=None):
    m, k = x.shape
    n = w.shape[1] if n is None else n
    bm, bn = min(bm, m), min(bn, n)
    assert col0 % bn == 0 and n % bn == 0 and col0 + n <= w.shape[1]
    first = col0 // bn
    return pl.pallas_call(
        _mm_kernel,
        grid=(n // bn, m // bm),
        in_specs=[pl.BlockSpec((bm, k), lambda j, i: (i, 0)),
                  pl.BlockSpec((k, bn), lambda j, i: (0, first + j))],
        out_specs=pl.BlockSpec((bm, bn), lambda j, i: (i, j)),
        out_shape=jax.ShapeDtypeStruct((m, n), F32),
        compiler_params=_params("parallel", "parallel"),
        name=name,
    )(x, w)


def _mm_dual_kernel(x_ref, w_ref, o_ref, ob_ref):
    r = _dot(x_ref[...], w_ref[...])
    o_ref[...] = r
    ob_ref[...] = r.astype(BF16)


def _mm_nt_kernel(wt_ref, x_ref, o_ref, xb_ref, *, scale):
    xb = x_ref[...].astype(BF16)
    xb_ref[...] = xb
    o_ref[...] = (_dot_nt(wt_ref[...], xb) * scale).astype(BF16)


def _mm_vt_kernel(x_ref, w_ref, o_ref, vt_ref):
    r = _dot(x_ref[...], w_ref[...])
    o_ref[...] = r
    for h in range(vt_ref.shape[0]):
        for kb in range(vt_ref.shape[1]):
            tile = r[kb * SB_TK:(kb + 1) * SB_TK, h * SB_HEAD_DIM:(h + 1) * SB_HEAD_DIM]
            vt_ref[h, kb] = tile.T.astype(BF16)


def _attention_projections(x, wq_t, w, k_col, v_col, bm, tag):
    m, k = x.shape
    n = wq_t.shape[0]
    rows = pl.BlockSpec((bm, k), lambda i: (i, 0))
    out = pl.BlockSpec((bm, n), lambda i: (i, 0))
    par = _params("parallel")
    qt, xb = pl.pallas_call(
        functools.partial(_mm_nt_kernel, scale=-(SB_HEAD_DIM ** -0.5) * LOG2E),
        grid=(m // bm,),
        in_specs=[pl.BlockSpec((n, k), lambda i: (0, 0)), rows],
        out_specs=[pl.BlockSpec((n, bm), lambda i: (0, i)), rows],
        out_shape=[jax.ShapeDtypeStruct((n, m), BF16), jax.ShapeDtypeStruct((m, k), BF16)],
        compiler_params=par, name=f"proj_qt_{tag}",
    )(wq_t, x)
    kf, kb = pl.pallas_call(
        _mm_dual_kernel,
        grid=(m // bm,),
        in_specs=[rows, pl.BlockSpec((k, n), lambda i: (0, k_col))],
        out_specs=[out, out],
        out_shape=[jax.ShapeDtypeStruct((m, n), F32), jax.ShapeDtypeStruct((m, n), BF16)],
        compiler_params=par, name=f"proj_k_{tag}",
    )(xb, w)
    heads = n // SB_HEAD_DIM
    vf, vt = pl.pallas_call(
        _mm_vt_kernel,
        grid=(m // bm,),
        in_specs=[rows, pl.BlockSpec((k, n), lambda i: (0, v_col))],
        out_specs=[out, pl.BlockSpec((heads, bm // SB_TK, SB_HEAD_DIM, SB_TK), lambda i: (0, i, 0, 0))],
        out_shape=[jax.ShapeDtypeStruct((m, n), F32),
                   jax.ShapeDtypeStruct((heads, m // SB_TK, SB_HEAD_DIM, SB_TK), BF16)],
        compiler_params=par, name=f"proj_v_{tag}",
    )(xb, w)
    return xb, qt, kf, kb, vf, vt


def _gdn_pre_kernel(cur_ref, halo_ref, prev_ref, cw_ref, ba_ref, gp_ref, q_ref, k_ref, v_ref, bg_ref, ext_scr):
    j = pl.program_id(1)
    bm = cur_ref.shape[0]
    w = GDN_HEADS * GDN_DIM
    for blk in range(3 * GDN_HEADS):
        sl = slice(blk * GDN_DIM, (blk + 1) * GDN_DIM)
        cur = cur_ref[:, sl]
        ext_scr[0:8, :] = jnp.where(j == 0, prev_ref[:, sl], halo_ref[:, sl])
        ext_scr[8:, :] = cur
        conv = cur * cw_ref[CONV_WIDTH - 1:CONV_WIDTH, sl]
        for d in range(1, CONV_WIDTH):
            conv = conv + ext_scr[pl.ds(8 - d, bm), :] * cw_ref[CONV_WIDTH - 1 - d:CONV_WIDTH - d, sl]
        c = conv * jax.nn.sigmoid(conv)
        osl = slice(sl.start % w, sl.start % w + GDN_DIM)
        if blk < GDN_HEADS:
            q_ref[:, osl] = c * lax.rsqrt(jnp.sum(c * c, axis=-1, keepdims=True) + L2_EPS) * (GDN_DIM ** -0.5)
        elif blk < 2 * GDN_HEADS:
            k_ref[:, osl] = c * lax.rsqrt(jnp.sum(c * c, axis=-1, keepdims=True) + L2_EPS)
        else:
            v_ref[:, osl] = c
    ba = ba_ref[...]
    gp = gp_ref[...]
    lane = lax.broadcasted_iota(jnp.int32, ba.shape, 1)
    bg_ref[...] = jnp.where(lane < GDN_HEADS, jax.nn.sigmoid(ba), gp[0:1] * _softplus(ba + gp[1:2]))


def _gdn_pre(conv_in, conv_prev8, conv_w, ba, gparams, bm, name):
    nb, t, c3 = conv_in.shape
    w = c3 // 3
    bm = min(bm, t)
    hb = bm // 8
    row = lambda b, j: (b, j, 0)
    out = jax.ShapeDtypeStruct((nb, t, w), F32)
    return pl.pallas_call(
        _gdn_pre_kernel,
        grid=(nb, t // bm),
        in_specs=[pl.BlockSpec((None, bm, c3), row),
                  pl.BlockSpec((None, 8, c3), lambda b, j: (b, jnp.maximum(j * hb - 1, 0), 0)),
                  pl.BlockSpec((None, 8, c3), lambda b, j: (b, 0, 0)),
                  pl.BlockSpec((CONV_WIDTH, c3), lambda b, j: (0, 0)),
                  pl.BlockSpec((None, bm, LANES), row),
                  pl.BlockSpec((8, LANES), lambda b, j: (0, 0))],
        out_specs=[pl.BlockSpec((None, bm, w), row)] * 3 + [pl.BlockSpec((None, bm, LANES), row)],
        out_shape=[out] * 3 + [jax.ShapeDtypeStruct((nb, t, LANES), F32)],
        scratch_shapes=[pltpu.VMEM((bm + 8, GDN_DIM), F32)],
        compiler_params=_params("parallel", "parallel"),
        name=name,
    )(conv_in, conv_in, conv_prev8, conv_w, ba, gparams)


def _gdn_prep_kernel(q_ref, k_ref, v_ref, bg_ref, u_ref, w_ref, qd_ref, kd_ref, intra_ref, gl_ref,
                     *, chunks):
    n = GDN_CHUNK
    ri = lax.broadcasted_iota(jnp.int32, (n, n), 0)
    ci = lax.broadcasted_iota(jnp.int32, (n, n), 1)
    incl = ri >= ci
    strict = ri > ci
    lane_pad = jnp.zeros((n, GDN_DIM - n), F32)
    rowid = lax.broadcasted_iota(jnp.int32, (n, GDN_DIM), 0)
    unroll = GDN_PREP_UNROLL if chunks % GDN_PREP_UNROLL == 0 else 1

    def chunk(cc, carry):
        cs = [cc * unroll + d for d in range(unroll)]
        hs = range(GDN_HEADS * unroll)
        sls = [slice((h % GDN_HEADS) * GDN_DIM, (h % GDN_HEADS + 1) * GDN_DIM) for h in hs]
        rws = [pl.ds(pl.multiple_of(cs[h // GDN_HEADS] * n, n), n) for h in hs]
        ks = [k_ref[r, sl] for r, sl in zip(rws, sls)]
        bgs = [bg_ref[r, :] for r in rws[::GDN_HEADS]]
        lanes = lambda col: jnp.broadcast_to(col, (n, GDN_DIM))
        betas = [lanes(bgs[h // GDN_HEADS][:, h % GDN_HEADS:h % GDN_HEADS + 1]) for h in hs]
        gs = [lanes(bgs[h // GDN_HEADS][:, GDN_HEADS + h % GDN_HEADS:GDN_HEADS + h % GDN_HEADS + 1]) for h in hs]
        gcums = [_prefix_rows(g, rowid) for g in gs]
        grows = [jnp.broadcast_to(gc.T[0:1, :], (n, n)) for gc in gcums]
        kbfs = [k.astype(BF16) for k in ks]
        kbs = [k * beta for k, beta in zip(ks, betas)]
        kks = [_dot_nt(kb.astype(BF16), kbf) for kb, kbf in zip(kbs, kbfs)]
        qks = [_dot_nt(q_ref[r, sl].astype(BF16), kbf) for r, sl, kbf in zip(rws, sls, kbfs)]
        gams = [jnp.where(incl, jnp.exp(jnp.where(incl, gc[:, :n] - gr, 0.0)), 0.0) for gc, gr in zip(gcums, grows)]
        egs = [jnp.exp(gc) for gc in gcums]
        mps = [-jnp.where(strict, kk * gam, 0.0) for kk, gam in zip(kks, gams)]
        sols = [jnp.concatenate([v_ref[r, sl] * beta, kb * eg], axis=-1)
                for r, sl, beta, kb, eg in zip(rws, sls, betas, kbs, egs)]
        span = 1
        while span < n:
            mpbs = [mp.astype(BF16) for mp in mps]
            if span < (1 << GDN_SPLIT_STEPS):
                upds = [_dot(mpb, jnp.concatenate(_split2(sol), axis=-1)) for mpb, sol in zip(mpbs, sols)]
                sols = [sol + (upd[:, :2 * GDN_DIM] + upd[:, 2 * GDN_DIM:]) for sol, upd in zip(sols, upds)]
            else:
                sols = [sol + _dot(mpb, sol.astype(BF16)) for mpb, sol in zip(mpbs, sols)]
            span *= 2
            if span < n:
                mps = [_dot(mpb, mpb) for mpb in mpbs]
        for h in hs:
            sl, rows, head = sls[h], rws[h], h % GDN_HEADS
            glast = gcums[h][n - 1:n, :]
            u_ref[rows, sl] = sols[h][:, :GDN_DIM]
            w_ref[rows, sl] = sols[h][:, GDN_DIM:].astype(BF16)
            intra = jnp.where(incl, qks[h] * gams[h], 0.0)
            intra_ref[rows, sl] = jnp.concatenate([intra, lane_pad], axis=-1).astype(BF16)
            qd_ref[rows, sl] = (q_ref[rows, sl] * egs[h]).astype(BF16)
            kd_ref[rows, sl] = (ks[h] * jnp.exp(glast - gcums[h])).astype(BF16)
            gl_ref[cs[h // GDN_HEADS], head:head + 1, :] = jnp.exp(glast)
        return carry

    lax.fori_loop(0, chunks // unroll, chunk, 0)


def _gdn_scan_kernel(u_ref, w_ref, qd_ref, kd_ref, intra_ref, gl_ref, s0_ref, o_ref, sfin_ref, s_scr, *, chunks):
    n = GDN_CHUNK
    cg = pl.program_id(1)

    @pl.when(cg == 0)
    def _():
        s_scr[...] = s0_ref[...]

    def chunk(c, carry):
        rows = pl.ds(pl.multiple_of(c * n, n), n)
        hs = range(GDN_HEADS)
        sls = [slice(h * GDN_DIM, (h + 1) * GDN_DIM) for h in hs]
        ss = [s_scr[h] for h in hs]
        rs = [_dot(jnp.concatenate([w_ref[rows, sl], qd_ref[rows, sl]], axis=0), s.astype(BF16))
              for sl, s in zip(sls, ss)]
        vbs = [(u_ref[rows, sl] - r[:n]).astype(BF16) for sl, r in zip(sls, rs)]
        upds = [_dot_tn(kd_ref[rows, sl], vb) for sl, vb in zip(sls, vbs)]
        for h in hs:
            s_scr[h] = ss[h] * gl_ref[c, h:h + 1, :] + upds[h]
        for h in hs:
            o_ref[rows, sls[h]] = rs[h][n:] + _dot(intra_ref[rows, h * GDN_DIM:h * GDN_DIM + n], vbs[h])
        return carry

    lax.fori_loop(0, chunks, chunk, 0)

    @pl.when(cg == pl.num_programs(1) - 1)
    def _():
        sfin_ref[...] = s_scr[...]


def _gdn(q, k, v, bg, s0, tag):
    nb, t, w = q.shape
    nc = t // GDN_CHUNK
    cgs = min(GDN_CHUNK_GROUP, nc)
    rows = cgs * GDN_CHUNK
    tok = pl.BlockSpec((None, rows, w), lambda b, c: (b, c, 0))
    gl = pl.BlockSpec((None, cgs, GDN_HEADS, GDN_DIM), lambda b, c: (b, c, 0, 0))
    st = pl.BlockSpec((None, GDN_HEADS, GDN_DIM, GDN_DIM), lambda b, c: (b, 0, 0, 0))
    tok_f32 = jax.ShapeDtypeStruct((nb, t, w), F32)
    tok_bf16 = jax.ShapeDtypeStruct((nb, t, w), BF16)
    gl_shape = jax.ShapeDtypeStruct((nb, nc, GDN_HEADS, GDN_DIM), F32)
    u, wy, qd, kd, intra, glast = pl.pallas_call(
        functools.partial(_gdn_prep_kernel, chunks=cgs),
        grid=(nb, nc // cgs),
        in_specs=[tok] * 3 + [pl.BlockSpec((None, rows, LANES), lambda b, c: (b, c, 0))],
        out_specs=[tok] * 5 + [gl],
        out_shape=[tok_f32] + [tok_bf16] * 4 + [gl_shape],
        compiler_params=_params("parallel", "parallel"),
        name=f"gdn_prep_{tag}",
    )(q, k, v, bg)
    return pl.pallas_call(
        functools.partial(_gdn_scan_kernel, chunks=cgs),
        grid=(nb, nc // cgs),
        in_specs=[tok] * 5 + [gl, st],
        out_specs=[tok, st],
        out_shape=[tok_f32, jax.ShapeDtypeStruct((nb, GDN_HEADS, GDN_DIM, GDN_DIM), F32)],
        scratch_shapes=[pltpu.VMEM((GDN_HEADS, GDN_DIM, GDN_DIM), F32)],
        compiler_params=_params("parallel", "arbitrary"),
        name=f"gdn_scan_{tag}",
    )(u, wy, qd, kd, intra, glast, s0)


def _sbp_kernel(bias_ref, qt_ref, k_ref, vt_ref, u2_ref, o_ref, acc_ref):
    hg = pl.program_id(0)
    i = pl.program_id(1)
    nh = qt_ref.shape[0]
    u2 = u2_ref[...]
    tk, tq = SB_TK, SB_TQ

    def blocks(js, accs, offs):
        valid = [None if o is None else
                 (lax.broadcasted_iota(jnp.int32, (tk, tq), 0) + o) < lax.broadcasted_iota(jnp.int32, (tk, tq), 1)
                 for o in offs]
        chains = [(b, h) for b in range(len(js)) for h in range(nh)]
        rows = [pl.ds(pl.multiple_of(j * tk, tk), tk) for j in js]
        accs = list(accs)
        nz, lg, rinc = {}, {}, {}

        def st_qk(c):
            b, h = c
            nz[c] = _dot(k_ref[rows[b], h * SB_HEAD_DIM:(h + 1) * SB_HEAD_DIM], qt_ref[h]) + bias_ref[hg * nh + h]

        def st_lg(c):
            v = _log2_one_minus_sigmoid(nz[c])
            lg[c] = v if valid[c[0]] is None else jnp.where(valid[c[0]], v, 0.0)

        def st_sum(c):
            rinc[c] = _dot(u2, jnp.concatenate(_split2(lg.pop(c)), axis=0))

        def st_av(c):
            b, h = c
            a = jnp.exp2((rinc[c] + accs[h]) - nz.pop(c))
            if valid[b] is not None:
                a = jnp.where(valid[b], a, 0.0)
            acc_ref[h] += _dot(vt_ref[h, js[b]], a.astype(BF16))
            accs[h] = accs[h] + rinc.pop(c)[0:1, :]

        stages = [st_qk, st_lg, st_sum, st_av]
        for step in range(len(chains) + (len(stages) - 1) * SB_SKEW):
            for si, st in enumerate(stages):
                ci = step - si * SB_SKEW
                if 0 <= ci < len(chains):
                    st(chains[ci])
        return tuple(accs)

    acc_ref[...] = jnp.zeros_like(acc_ref)
    per = tq // tk
    nb = SB_KEY_BLOCKS
    first = i * per
    accs = blocks([first + per - 1 - d for d in range(per)], tuple(jnp.zeros((1, tq), F32) for _ in range(nh)),
                  [(per - 1 - d) * tk for d in range(per)])
    accs = lax.fori_loop(0, first // nb,
                         lambda n, c: blocks([first - 1 - nb * n - d for d in range(nb)], c, [None] * nb), accs)
    done = (first // nb) * nb
    size = nb // 2
    while size >= per:
        take = ((first - done) // size) % 2
        start = first - done - 1
        accs = lax.fori_loop(0, take, lambda n, c, start=start, size=size:
                             blocks([start - d for d in range(size)], c, [None] * size), accs)
        done = done + take * size
        size //= 2
    for h in range(nh):
        o_ref[:, h * SB_HEAD_DIM:(h + 1) * SB_HEAD_DIM] = acc_ref[h].T.astype(BF16)


def _sb_prompt(qt, kb, vt, bias):
    hh, dh, s = qt.shape
    hg = SB_HEAD_GROUP
    nkb = s // SB_TK
    r = jnp.arange(SB_TK)
    u = (r[None, :] >= r[:, None]).astype(BF16)
    u2 = jnp.concatenate([u, u], axis=1)
    once = pl.Buffered(1)
    return pl.pallas_call(
        _sbp_kernel,
        grid=(hh // hg, s // SB_TQ),
        in_specs=[pl.BlockSpec(memory_space=pltpu.SMEM),
                  pl.BlockSpec((hg, dh, SB_TQ), lambda g, i: (g, 0, i)),
                  pl.BlockSpec((s, hg * dh), lambda g, i: (0, g), pipeline_mode=once),
                  pl.BlockSpec((hg, nkb, dh, SB_TK), lambda g, i: (g, 0, 0, 0), pipeline_mode=once),
                  pl.BlockSpec((SB_TK, 2 * SB_TK), lambda g, i: (0, 0), pipeline_mode=once)],
        out_specs=pl.BlockSpec((SB_TQ, hg * dh), lambda g, i: (i, g)),
        out_shape=jax.ShapeDtypeStruct((s, hh * dh), BF16),
        scratch_shapes=[pltpu.VMEM((hg, dh, SB_TQ), F32)],
        compiler_params=_params("parallel", "arbitrary"),
        name="sb_prompt",
    )(bias.astype(F32) * (-LOG2E), qt, kb, vt, u2)


def _sbs_kernel(pt_ref, qbd_ref, bias_ref, knew_ref, vnew_ref, *rest, pages):
    kc_refs, vc_refs = rest[:pages], rest[pages:2 * pages]
    u2_ref, o_ref, acc_scr, out_scr = rest[2 * pages:]
    p = pl.program_id(1)
    qbd = qbd_ref[...]
    bias = bias_ref[...]
    u2 = u2_ref[...]
    nrow, npos = bias.shape
    t = nrow // SB_HEADS

    def page_rows(ref):
        if ref.shape[1] != SB_HEAD_DIM:
            return ref[...].astype(BF16)
        return jnp.concatenate([ref[pl.ds(h, npos, stride=SB_HEADS), :].astype(BF16) for h in range(SB_HEADS)],
                               axis=1)

    def group(k_refs, v_refs, masked):
        kcat = jnp.concatenate([page_rows(r) for r in k_refs], axis=0)
        vcat = jnp.concatenate([page_rows(r) for r in v_refs], axis=0)
        zall = _dot_nt(qbd, kcat)
        acc = acc_scr[...]
        probs = []
        for gi in range(len(k_refs)):
            z = zall[:, gi * npos:(gi + 1) * npos] + bias
            lg = _neg_softplus(z)
            if masked:
                tok = lax.broadcasted_iota(jnp.int32, (nrow, npos), 0) % t
                valid = lax.broadcasted_iota(jnp.int32, (nrow, npos), 1) < tok
                lg = jnp.where(valid, lg, 0.0)
            hi, lo = _split2(lg)
            rinc = _dot(jnp.concatenate([hi, lo], axis=1), u2)
            a = jnp.exp(z + rinc + acc)
            if masked:
                a = jnp.where(valid, a, 0.0)
            probs.append(a.astype(BF16))
            acc = acc + jnp.sum(lg, axis=-1, keepdims=True)
        out_scr[...] += _dot(jnp.concatenate(probs, axis=1), vcat)
        acc_scr[...] = acc

    @pl.when(p == 0)
    def _():
        acc_scr[...] = jnp.zeros_like(acc_scr)
        out_scr[...] = jnp.zeros_like(out_scr)
        group([knew_ref], [vnew_ref], True)

    group(kc_refs, vc_refs, False)

    @pl.when(p == pl.num_programs(1) - 1)
    def _():
        for h in range(SB_HEADS):
            sl = slice(h * SB_HEAD_DIM, (h + 1) * SB_HEAD_DIM)
            o_ref[:, sl] = out_scr[h * t:(h + 1) * t, sl]


def _sb_sample(q, k, v, bias, cache_k, cache_v, page_ids):
    nb, t, w = q.shape
    hh, dh = SB_HEADS, SB_HEAD_DIM
    npg = page_ids.shape[1]
    psz = cache_k.shape[1] // hh
    pgs = min(SB_PAGE_GROUP, npg)
    qs = (q * (dh ** -0.5)).astype(BF16).reshape(nb, t, hh, dh)
    eye = jnp.eye(hh, dtype=BF16)
    qbd = (qs.transpose(0, 2, 1, 3)[:, :, :, None, :] * eye[None, :, None, :, None]).reshape(nb, hh * t, w)
    bias_rep = jnp.broadcast_to(jnp.repeat(bias.astype(F32), t)[:, None], (hh * t, psz))
    pad = ((0, 0), (0, psz - t), (0, 0))
    knew, vnew = jnp.pad(k, pad), jnp.pad(v, pad)
    r = jnp.arange(psz)
    u = (r[:, None] >= r[None, :]).astype(BF16)
    u2 = jnp.concatenate([u, u], axis=0)
    per_seq = lambda b, p, pt: (b, 0, 0)
    cache = [pl.BlockSpec((None, psz * hh, dh), lambda b, p, pt, gi=gi: (pt[b, npg - 1 - (p * pgs + gi)], 0, 0))
             for gi in range(pgs)]
    return pl.pallas_call(
        functools.partial(_sbs_kernel, pages=pgs),
        grid_spec=pltpu.PrefetchScalarGridSpec(
            num_scalar_prefetch=1,
            grid=(nb, npg // pgs),
            in_specs=[pl.BlockSpec((None, hh * t, w), per_seq),
                      pl.BlockSpec((hh * t, psz), lambda b, p, pt: (0, 0)),
                      pl.BlockSpec((None, psz, w), per_seq),
                      pl.BlockSpec((None, psz, w), per_seq)]
                     + cache + cache
                     + [pl.BlockSpec((2 * psz, psz), lambda b, p, pt: (0, 0))],
            out_specs=pl.BlockSpec((None, t, w), per_seq),
            scratch_shapes=[pltpu.VMEM((hh * t, psz), F32), pltpu.VMEM((hh * t, w), F32)],
        ),
        out_shape=jax.ShapeDtypeStruct((nb, t, w), F32),
        compiler_params=_params("parallel", "arbitrary"),
        name="sb_sample",
    )(page_ids, qbd, bias_rep, knew, vnew, *([cache_k] * pgs), *([cache_v] * pgs), u2)


def _merge_kernel(oa_ref, ob_ref, zg_ref, ga_ref, gb_ref, x_ref, nw_ref, woa_ref, wob_ref, wout_ref,
                  g1_ref, b1_ref, h_ref, hb_ref, *, alpha):
    ob = ob_ref[...]
    zg = zg_ref[...]
    nw = nw_ref[...]
    parts = []
    for h in range(GDN_HEADS):
        sl = slice(h * GDN_DIM, (h + 1) * GDN_DIM)
        oh = ob[:, sl]
        zh = zg[:, sl]
        rn = oh * lax.rsqrt(jnp.mean(oh * oh, axis=-1, keepdims=True) + RMS_EPS)
        parts.append((rn * nw * (zh * jax.nn.sigmoid(zh))).astype(BF16))
    obn = jnp.concatenate(parts, axis=-1)
    merged = (jax.nn.sigmoid(ga_ref[...]) * _dot(oa_ref[...], woa_ref[...])
              + jax.nn.sigmoid(gb_ref[...]) * _dot(obn, wob_ref[...]))
    y = alpha * x_ref[...] + _dot(merged.astype(BF16), wout_ref[...])
    hn = _layernorm(y, g1_ref[...], b1_ref[...])
    h_ref[...] = hn
    hb_ref[...] = hn.astype(BF16)


def _merge(oa, ob, zg, gab, x, nw, woa, wob, wout, g1, b1, alpha, bm, name):
    m, d = x.shape
    w = oa.shape[1]
    bm = min(bm, m)
    row = lambda i: (i, 0)
    fix = lambda i: (0, 0)
    once = functools.partial(pl.BlockSpec, index_map=fix, pipeline_mode=pl.Buffered(1))
    return pl.pallas_call(
        functools.partial(_merge_kernel, alpha=alpha),
        grid=(m // bm,),
        in_specs=[pl.BlockSpec((bm, w), row), pl.BlockSpec((bm, w), row), pl.BlockSpec((bm, w), row),
                  pl.BlockSpec((bm, d), row), pl.BlockSpec((bm, d), lambda i: (i, 1)), pl.BlockSpec((bm, d), row),
                  pl.BlockSpec((1, GDN_DIM), fix),
                  once((w, d)), once((w, d)), once((d, d)),
                  pl.BlockSpec((1, d), fix), pl.BlockSpec((1, d), fix)],
        out_specs=[pl.BlockSpec((bm, d), row), pl.BlockSpec((bm, d), row)],
        out_shape=[jax.ShapeDtypeStruct((m, d), F32), jax.ShapeDtypeStruct((m, d), BF16)],
        compiler_params=_params("parallel"),
        name=name,
    )(oa, ob, zg, gab, gab, x, nw, woa, wob, wout, g1, b1)


def _ffn_kernel(hb_ref, h_ref, wgu_ref, wd_ref, g2_ref, b2_ref, y_ref, acc_ref, *, alpha):
    f = pl.program_id(1)

    @pl.when(f == 0)
    def _():
        acc_ref[...] = jnp.zeros_like(acc_ref)

    gu = _dot(hb_ref[...], wgu_ref[...])
    bf = gu.shape[1] // 2
    gate, up = gu[:, :bf], gu[:, bf:]
    act = (gate * jax.nn.sigmoid(gate) * up).astype(BF16)
    acc_ref[...] += _dot(act, wd_ref[...])

    @pl.when(f == pl.num_programs(1) - 1)
    def _():
        y_ref[...] = _layernorm(alpha * h_ref[...] + acc_ref[...], g2_ref[...], b2_ref[...])


def _ffn(hb, h, wgu, wd, g2, b2, alpha, bm, bf, name):
    m, d = h.shape
    dff = wd.shape[0]
    nf = dff // bf
    bm = min(bm, m)
    row = lambda i, f: (i, 0)
    fix = lambda i, f: (0, 0)
    return pl.pallas_call(
        functools.partial(_ffn_kernel, alpha=alpha),
        grid=(m // bm, nf),
        in_specs=[pl.BlockSpec((bm, d), row), pl.BlockSpec((bm, d), row),
                  pl.BlockSpec((d, 2 * bf), lambda i, f: (0, f)),
                  pl.BlockSpec((bf, d), lambda i, f: (f, 0)),
                  pl.BlockSpec((1, d), fix), pl.BlockSpec((1, d), fix)],
        out_specs=pl.BlockSpec((bm, d), row),
        out_shape=jax.ShapeDtypeStruct((m, d), F32),
        scratch_shapes=[pltpu.VMEM((bm, d), F32)],
        compiler_params=_params("parallel", "arbitrary"),
        name=name,
    )(hb, h, wgu.reshape(d, 2, nf, bf).transpose(0, 2, 1, 3).reshape(d, 2 * dff), wd, g2, b2)


def _layer(x, sb_mix, conv_prev, s_prev, wts, alpha, tag, bm):
    nb, t, d = x.shape
    m = nb * t
    w = SB_HEADS * SB_HEAD_DIM
    xf = x.reshape(m, d)
    cols = wts["in_cols"]

    def project(xb, nm):
        col0, n, bn = cols[nm]
        return _matmul(xb, wts["w_main"], bm, bn, f"proj_{nm}_{tag}", col0, n)

    xb, o_a, ka, va = sb_mix(xf, project)
    conv_in = project(xb, "conv").reshape(nb, t, 3 * w)
    zg, ba = project(xb, "z"), project(xb, "ba")
    gab = _matmul(xb, wts["w_gab"], bm, 1024, f"proj_gab_{tag}")

    prev8 = jnp.pad(conv_prev.astype(F32), ((0, 0), (8 - (CONV_WIDTH - 1), 0), (0, 0)))
    qn, kn, vv, bg = _gdn_pre(conv_in, prev8, wts["conv_w"], ba.reshape(nb, t, LANES), wts["gparams"],
                                   256, f"gdn_pre_{tag}")
    tp = -(-t // GDN_CHUNK) * GDN_CHUNK
    if tp != t:
        padt = lambda a: jnp.pad(a, ((0, 0), (0, tp - t), (0, 0)))
        qn, kn, vv, bg = map(padt, (qn, kn, vv, bg))
    o_b, s_new = _gdn(qn, kn, vv, bg, s_prev.astype(F32), tag)
    o_b = o_b[:, :t].reshape(m, w)

    h, hb = _merge(o_a, o_b, zg, gab, xf, wts["norm_w"], wts["w_o_a"], wts["w_o_b"], wts["w_out"],
                   wts["ln1_g"], wts["ln1_b"], alpha, 256, f"merge_{tag}")
    y = _ffn(hb, h, wts["w_gu"], wts["w_down"], wts["ln2_g"], wts["ln2_b"], alpha, 512, 512, f"ffn_{tag}")
    conv_new = jnp.concatenate([conv_prev.astype(F32), conv_in], axis=1)[:, -(CONV_WIDTH - 1):]
    return (y.reshape(nb, t, d),
            (ka.reshape(nb, t, SB_HEADS, SB_HEAD_DIM), va.reshape(nb, t, SB_HEADS, SB_HEAD_DIM), s_new, conv_new))


def _prep_weights(w_in, sb_bias, conv_w, a_log, dt_bias, gdn_norm_w, w_o_a, w_o_b, w_out,
                  ln1_g, ln1_b, w_gu, w_down, ln2_g, ln2_b):
    w = SB_HEADS * SB_HEAD_DIM
    d = w_in.shape[0]
    hh = GDN_HEADS
    o = 0
    in_cols = {}
    for nm, width, bn in (("q", w, w), ("k", w, w), ("v", w, w), ("conv", 3 * w, w), ("z", w, w),
                          ("ba", 2 * hh, LANES)):
        in_cols[nm] = (o, max(width, bn), bn)
        o += width
    o_gab = o
    assert in_cols["ba"][0] + LANES <= w_in.shape[1] and o_gab + 2 * d == w_in.shape[1]
    gp = jnp.zeros((8, LANES), F32)
    gp = gp.at[0, hh:2 * hh].set(-jnp.exp(a_log.astype(F32))).at[1, hh:2 * hh].set(dt_bias.astype(F32))
    w_main = w_in.astype(BF16)
    return dict(
        in_cols=in_cols, w_main=w_main, w_gab=w_main[:, o_gab:], w_q_t=w_main[:, :w].T,
        sb_bias=sb_bias, conv_w=conv_w.astype(F32), gparams=gp,
        norm_w=gdn_norm_w.astype(F32).reshape(1, GDN_DIM),
        w_o_a=w_o_a.astype(BF16), w_o_b=w_o_b.astype(BF16), w_out=w_out.astype(BF16),
        ln1_g=ln1_g.reshape(1, d), ln1_b=ln1_b.reshape(1, d),
        w_gu=w_gu.astype(BF16), w_down=w_down.astype(BF16),
        ln2_g=ln2_g.reshape(1, d), ln2_b=ln2_b.reshape(1, d))


def kernel(x_prompt, x_sample, cache_k, cache_v, state_gdn, state_conv, page_table, w_in, sb_bias, conv_w,
           a_log, dt_bias, gdn_norm_w, w_o_a, w_o_b, w_out, ln1_g, ln1_b, w_gu, w_down, ln2_g, ln2_b):
    depth = w_in.shape[0]
    alpha = (2.0 * depth) ** 0.25
    w = SB_HEADS * SB_HEAD_DIM
    nphys, psz = cache_k.shape[1], cache_k.shape[2]
    ck = cache_k.reshape(depth * nphys, psz * SB_HEADS, SB_HEAD_DIM)
    cv = cache_v.reshape(depth * nphys, psz * SB_HEADS, SB_HEAD_DIM)
    yp, ys = x_prompt, x_sample
    outs = [[] for _ in range(8)]
    for l in range(depth):
        wts = _prep_weights(w_in[l], sb_bias[l], conv_w[l], a_log[l], dt_bias[l], gdn_norm_w[l], w_o_a[l],
                            w_o_b[l], w_out[l], ln1_g[l], ln1_b[l], w_gu[l], w_down[l], ln2_g[l], ln2_b[l])
        nbp, tp = yp.shape[0], yp.shape[1]
        conv0 = jnp.zeros((nbp, CONV_WIDTH - 1, 3 * w), F32)
        s0 = jnp.zeros((nbp, GDN_HEADS, GDN_DIM, GDN_DIM), F32)

        assert nbp == 1, "prompt attention treats all prompt rows as one sequence"

        def sb_p(xf, project):
            k_col, v_col = (wts["in_cols"][nm][0] // w for nm in ("k", "v"))
            xb, qt, kf, kb, vf, vt = _attention_projections(xf, wts["w_q_t"], wts["w_main"], k_col, v_col, 1024, "p")
            return xb, _sb_prompt(qt.reshape(SB_HEADS, SB_HEAD_DIM, tp), kb, vt, wts["sb_bias"]), kf, vf

        yp, (kp, vp, sp, cp) = _layer(yp, sb_p, conv0, s0, wts, alpha, "p", 1024)

        nbs, ts = ys.shape[0], ys.shape[1]
        page_ids = page_table + l * nphys

        def sb_s(xf, project):
            xb = xf.astype(BF16)
            q, k, v = project(xb, "q"), project(xb, "k"), project(xb, "v")
            r3 = lambda a: a.reshape(nbs, ts, w)
            o = _sb_sample(r3(q), r3(k), r3(v), wts["sb_bias"], ck, cv, page_ids)
            return xb, o.reshape(nbs * ts, w).astype(BF16), k, v

        ys, (ksm, vsm, ssm, csm) = _layer(ys, sb_s, state_conv[l], state_gdn[l], wts, alpha, "s", 256)
        for lst, val in zip(outs, (kp, vp, sp, cp, ksm, vsm, ssm, csm)):
            lst.append(val)
    return (yp, ys) + tuple(jnp.stack(o) for o in outs)
```
